```python
import jax, jax.numpy as jnp
from jax import lax
import numpy as np

D_MODEL = 1024
BATCH = 8
SEQ = 2048
DEPTH = 4
DEC_BATCH = 128
DEC_SEQ = 1
PAST_LEN = 8192
PAGE_SIZE = 128

N_A_LAYERS = DEPTH // 2
N_B_LAYERS = DEPTH - N_A_LAYERS
HEAD_DIM = 64
A_GROUPS = ((128, 1), (512, 4), (2048, 16))
N_GROUPS = len(A_GROUPS)
A_HEADS = D_MODEL // (2 * HEAD_DIM)
B_HEADS = D_MODEL // HEAD_DIM
B_KV_HEADS = 2
B_WINDOW = 128
D_FF = 2816
CONV_W = 3
ROPE_THETA = 10000.0
EPS = 1e-6
BLOCK = 128

kernel_name = "yoco_dilated_swa_sink_convffn_step"


def rms_norm(x, g):
    xf = x.astype(jnp.float32)
    y = xf * lax.rsqrt(jnp.mean(xf * xf, axis=-1, keepdims=True) + EPS)
    return (y * g.astype(jnp.float32)).astype(x.dtype)


def rope(x, pos):
    half = HEAD_DIM // 2
    inv_freq = ROPE_THETA ** (-jnp.arange(half, dtype=jnp.float32) / half)
    ang = pos.astype(jnp.float32)[:, None] * inv_freq[None, :]
    cos = jnp.cos(ang)[:, None, :]
    sin = jnp.sin(ang)[:, None, :]
    xf = x.astype(jnp.float32)
    x1, x2 = xf[..., :half], xf[..., half:]
    return jnp.concatenate([x1 * cos - x2 * sin, x2 * cos + x1 * sin], axis=-1).astype(x.dtype)


def softmax_lse(s):
    m = jnp.max(s, axis=-1, keepdims=True)
    e = jnp.exp(s - m)
    den = jnp.sum(e, axis=-1, keepdims=True)
    return e / den, (m + jnp.log(den))[..., 0]


def banded_attention(q, k, v, max_dist):
    n, L, H, hd = q.shape
    kvh = k.shape[2]
    rep = H // kvh
    blk = min(BLOCK, L)
    nb = -(-L // blk)
    pad = nb * blk - L
    qb = jnp.pad(q, ((0, 0), (0, pad), (0, 0), (0, 0))).reshape(n, nb, blk, kvh, rep, hd)
    span = max_dist + blk
    idx = jnp.arange(nb)[:, None] * blk + jnp.arange(span)[None, :]
    kpad = ((0, 0), (max_dist, pad), (0, 0), (0, 0))
    kb = jnp.pad(k, kpad)[:, idx]
    vb = jnp.pad(v, kpad)[:, idx]
    kpos = idx - max_dist
    qpos = jnp.arange(nb)[:, None] * blk + jnp.arange(blk)[None, :]
    dist = qpos[:, :, None] - kpos[:, None, :]
    valid = (dist >= 0) & (dist <= max_dist) & (kpos[:, None, :] >= 0)
    s = jnp.einsum('nbqgrd,nbkgd->nbgrqk', qb, kb).astype(jnp.float32) * (hd ** -0.5)
    s = jnp.where(valid[None, :, None, None], s, -jnp.inf)
    p, lse = softmax_lse(s)
    o = jnp.einsum('nbgrqk,nbkgd->nbqgrd', p.astype(v.dtype), vb)
    o = o.reshape(n, nb * blk, H, hd)[:, :L]
    lse = lse.transpose(0, 1, 4, 2, 3).reshape(n, nb * blk, H)[:, :L]
    return o, lse


def dilated_attention_prompt(q, k, v, window, dil):
    n, S, H, hd = q.shape
    L = S // dil
    def split(t):
        return t.reshape(n, L, dil, H, hd).transpose(0, 2, 1, 3, 4).reshape(n * dil, L, H, hd)
    o, lse = banded_attention(split(q), split(k), split(v), window // dil)
    o = o.reshape(n, dil, L, H, hd).transpose(0, 2, 1, 3, 4).reshape(n, S, H, hd)
    lse = lse.reshape(n, dil, L, H).transpose(0, 2, 1, 3).reshape(n, S, H)
    return o, lse


def dilated_attention_sample(q, k_new, v_new, kv_cache, window, dil):
    n, T, H, hd = q.shape
    wb = kv_cache.shape[2]
    kc = jnp.concatenate([kv_cache[:, 0].astype(k_new.dtype), k_new], axis=1)
    vc = jnp.concatenate([kv_cache[:, 1].astype(v_new.dtype), v_new], axis=1)
    m = jnp.arange(window // dil + 1)
    idx = wb + jnp.arange(T)[:, None] - m[None, :] * dil
    valid = idx >= 0
    idxc = jnp.maximum(idx, 0)
    kg = kc[:, idxc]
    vg = vc[:, idxc]
    s = jnp.einsum('nthd,ntmhd->nthm', q, kg).astype(jnp.float32) * (hd ** -0.5)
    s = jnp.where(valid[None, :, None, :], s, -jnp.inf)
    p, lse = softmax_lse(s)
    o = jnp.einsum('nthm,ntmhd->nthd', p.astype(vc.dtype), vg)
    new_cache = jnp.stack([kc[:, T:], vc[:, T:]], axis=1)
    return o, lse, new_cache


def a_mixer(hn, pos, caches, w_in, q_g, k_g, w_out):
    n, T, _ = hn.shape
    qkv = (hn @ w_in).reshape(n, T, 3, N_GROUPS, A_HEADS, HEAD_DIM)
    outs, lses, rows = [], [], []
    for g, (window, dil) in enumerate(A_GROUPS):
        q = rope(rms_norm(qkv[:, :, 0, g], q_g[g]), pos)
        k = rope(rms_norm(qkv[:, :, 1, g], k_g[g]), pos)
        v = qkv[:, :, 2, g]
        if caches is None:
            o, lse = dilated_attention_prompt(q, k, v, window, dil)
            keep = min(window, T)
            new = jnp.stack([k[:, T - keep:], v[:, T - keep:]], axis=1)
        else:
            o, lse, new = dilated_attention_sample(q, k, v, caches[g], window, dil)
        outs.append(o)
        lses.append(lse)
        rows.append(new)
    wts = jax.nn.softmax(jnp.stack(lses, axis=0), axis=0)
    o = jnp.einsum('gnthd,gnth->nthd', jnp.stack(outs, axis=0), wts.astype(outs[0].dtype))
    return o.reshape(n, T, A_HEADS * HEAD_DIM) @ w_out, rows


def shared_kv(h, pos, kv_norm_g, w_kv, k_norm_g):
    n, T, _ = h.shape
    kv = (rms_norm(h, kv_norm_g) @ w_kv).reshape(n, T, 2, B_KV_HEADS, HEAD_DIM)
    k = rope(rms_norm(kv[:, :, 0], k_norm_g), pos)
    return k, kv[:, :, 1]


def b_attention_sample(q, kc, vc, pos):
    n, T, H, hd = q.shape
    K = kc.shape[1]
    kpos = PAST_LEN + T - K + jnp.arange(K)
    dist = pos[:, None] - kpos[None, :]
    valid = (dist >= 0) & (dist < B_WINDOW)
    qg = q.reshape(n, T, B_KV_HEADS, H // B_KV_HEADS, hd)
    s = jnp.einsum('ntgrd,nkgd->ntgrk', qg, kc).astype(jnp.float32) * (hd ** -0.5)
    s = jnp.where(valid[None, :, None, None, :], s, -jnp.inf)
    p, lse = softmax_lse(s)
    o = jnp.einsum('ntgrk,nkgd->ntgrd', p.astype(vc.dtype), vc)
    return o.reshape(n, T, H, hd), lse.reshape(n, T, H)


def conv_ffn(hn, conv_state, w_up, conv_w, conv_b, w_down):
    T = hn.shape[1]
    u = hn @ w_up
    uc = jnp.concatenate([conv_state.astype(u.dtype), u], axis=1)
    c = sum((conv_w[j] * uc[:, j:j + T] for j in range(CONV_W)), conv_b)
    gate, val = jnp.split(c, 2, axis=-1)
    return (jax.nn.silu(gate) * val) @ w_down, uc[:, T:]


def trunk(x, sample, a_caches, b_cache, conv_state, attn_norm_g, ffn_norm_g, a_w_in, a_q_norm_g,
          a_k_norm_g, a_w_out, b_kv_norm_g, b_w_kv, b_k_norm_g, b_w_q, b_q_norm_g, b_sinks, b_w_out,
          ffn_w_up, ffn_conv_w, ffn_conv_b, ffn_w_down):
    n, T, _ = x.shape
    pos = (PAST_LEN if sample else 0) + jnp.arange(T)
    h = x
    new_a = [[] for _ in A_GROUPS]
    new_conv = []
    k_sh = v_sh = new_b = None
    for layer in range(DEPTH):
        hn = rms_norm(h, attn_norm_g[layer])
        if layer < N_A_LAYERS:
            caches = [c[layer] for c in a_caches] if sample else None
            mix, rows = a_mixer(hn, pos, caches, a_w_in[layer], a_q_norm_g[layer],
                                a_k_norm_g[layer], a_w_out[layer])
            for g in range(N_GROUPS):
                new_a[g].append(rows[g])
        else:
            if layer == N_A_LAYERS:
                k_sh, v_sh = shared_kv(h, pos, b_kv_norm_g, b_w_kv, b_k_norm_g)
                if sample:
                    k_sh = jnp.concatenate([b_cache[:, 0].astype(k_sh.dtype), k_sh], axis=1)
                    v_sh = jnp.concatenate([b_cache[:, 1].astype(v_sh.dtype), v_sh], axis=1)
                    new_b = jnp.stack([k_sh[:, T:], v_sh[:, T:]], axis=1)
                else:
                    keep = min(B_WINDOW, T)
                    new_b = jnp.stack([k_sh[:, T - keep:], v_sh[:, T - keep:]], axis=1)
            j = layer - N_A_LAYERS
            q = (hn @ b_w_q[j]).reshape(n, T, B_HEADS, HEAD_DIM)
            q = rope(rms_norm(q, b_q_norm_g[j]), pos)
            if sample:
                o, lse = b_attention_sample(q, k_sh, v_sh, pos)
            else:
                o, lse = banded_attention(q, k_sh, v_sh, B_WINDOW - 1)
            o = o * jax.nn.sigmoid(lse - b_sinks[j].astype(jnp.float32))[..., None].astype(o.dtype)
            mix = o.reshape(n, T, B_HEADS * HEAD_DIM) @ b_w_out[j]
        h = h + mix
        cs = conv_state[layer] if sample else jnp.zeros((n, CONV_W - 1, 2 * D_FF), h.dtype)
        f, ncs = conv_ffn(rms_norm(h, ffn_norm_g[layer]), cs, ffn_w_up[layer], ffn_conv_w[layer],
                          ffn_conv_b[layer], ffn_w_down[layer])
        h = h + f
        new_conv.append(ncs)
    a_out = [jnp.stack(r, axis=0) for r in new_a]
    return h, a_out, new_b, jnp.stack(new_conv, axis=0)


def setup_inputs(seed: int = 0) -> dict:
    key = jax.random.key(seed)
    ks = iter(jax.random.split(key, 40))

    def nrm(shape, scale):
        return scale * jax.random.normal(next(ks), shape, jnp.float32)

    def gain(shape):
        return 1.0 + 0.05 * jax.random.normal(next(ks), shape, jnp.float32)

    def a_cache(window):
        return nrm((N_A_LAYERS, DEC_BATCH, 2, min(window, PAST_LEN), A_HEADS, HEAD_DIM), 1.0)

    tap_bias = (jnp.arange(CONV_W) == CONV_W - 1).astype(jnp.float32)[None, :, None]
    return {
        "x_prompt": nrm((BATCH, SEQ, D_MODEL), 1.0),
        "x_sample": nrm((DEC_BATCH, DEC_SEQ, D_MODEL), 1.0),
        "cache_a_kv_w128": a_cache(A_GROUPS[0][0]),
        "cache_a_kv_w512": a_cache(A_GROUPS[1][0]),
        "cache_a_kv_w2048": a_cache(A_GROUPS[2][0]),
        "cache_b_kv": nrm((DEC_BATCH, 2, min(B_WINDOW, PAST_LEN), B_KV_HEADS, HEAD_DIM), 1.0),
        "state_ffn_conv": nrm((DEPTH, DEC_BATCH, CONV_W - 1, 2 * D_FF), 1.0),
        "attn_norm_g": gain((DEPTH, D_MODEL)),
        "ffn_norm_g": gain((DEPTH, D_MODEL)),
        "a_w_in": nrm((N_A_LAYERS, D_MODEL, 3 * N_GROUPS * A_HEADS * HEAD_DIM), D_MODEL ** -0.5),
        "a_q_norm_g": gain((N_A_LAYERS, N_GROUPS, HEAD_DIM)),
        "a_k_norm_g": gain((N_A_LAYERS, N_GROUPS, HEAD_DIM)),
        "a_w_out": nrm((N_A_LAYERS, A_HEADS * HEAD_DIM, D_MODEL), (A_HEADS * HEAD_DIM) ** -0.5),
        "b_kv_norm_g": gain((D_MODEL,)),
        "b_w_kv": nrm((D_MODEL, 2 * B_KV_HEADS * HEAD_DIM), D_MODEL ** -0.5),
        "b_k_norm_g": gain((HEAD_DIM,)),
        "b_w_q": nrm((N_B_LAYERS, D_MODEL, B_HEADS * HEAD_DIM), D_MODEL ** -0.5),
        "b_q_norm_g": gain((N_B_LAYERS, HEAD_DIM)),
        "b_sinks": nrm((N_B_LAYERS, B_HEADS), 1.0),
        "b_w_out": nrm((N_B_LAYERS, B_HEADS * HEAD_DIM, D_MODEL), (B_HEADS * HEAD_DIM) ** -0.5),
        "ffn_w_up": nrm((DEPTH, D_MODEL, 2 * D_FF), D_MODEL ** -0.5),
        "ffn_conv_w": nrm((DEPTH, CONV_W, 2 * D_FF), 0.3) + tap_bias,
        "ffn_conv_b": nrm((DEPTH, 2 * D_FF), 0.02),
        "ffn_w_down": nrm((DEPTH, D_FF, D_MODEL), D_FF ** -0.5),
    }


def reference(x_prompt, x_sample, cache_a_kv_w128, cache_a_kv_w512, cache_a_kv_w2048, cache_b_kv,
              state_ffn_conv, attn_norm_g, ffn_norm_g, a_w_in, a_q_norm_g, a_k_norm_g, a_w_out,
              b_kv_norm_g, b_w_kv, b_k_norm_g, b_w_q, b_q_norm_g, b_sinks, b_w_out,
              ffn_w_up, ffn_conv_w, ffn_conv_b, ffn_w_down):
    weights = (attn_norm_g, ffn_norm_g, a_w_in, a_q_norm_g, a_k_norm_g, a_w_out, b_kv_norm_g,
               b_w_kv, b_k_norm_g, b_w_q, b_q_norm_g, b_sinks, b_w_out,
               ffn_w_up, ffn_conv_w, ffn_conv_b, ffn_w_down)
    y_prompt, a_p, b_p, conv_p = trunk(x_prompt, False, None, None, None, *weights)
    y_sample, a_s, b_s, conv_s = trunk(
        x_sample, True, (cache_a_kv_w128, cache_a_kv_w512, cache_a_kv_w2048),
        cache_b_kv, state_ffn_conv, *weights)
    return (y_prompt, y_sample, a_p[0], a_s[0], a_p[1], a_s[1], a_p[2], a_s[2], b_p, b_s, conv_p, conv_s)
```

```python
import functools

import jax
import jax.numpy as jnp
from jax import lax
from jax.experimental import pallas as pl
from jax.experimental.pallas import tpu as pltpu

D_MODEL = 1024
HEAD_DIM = 64
N_GROUPS = 3
A_DILS = (1, 4, 16)
A_WINDOWS = (128, 512, 2048)
A_HEADS = 8
A_WIDTH = A_HEADS * HEAD_DIM
B_HEADS = 16
B_KV_HEADS = 2
B_WINDOW = 128
D_FF = 2816
ROPE_THETA = 10000.0
EPS = 1e-6
PAST_LEN = 8192
LANES = 128
BLK = 128
NEG = -1e30
VMEM_LIMIT = 56 * 1024 * 1024

F32 = jnp.float32
BF16 = jnp.bfloat16
_NT = (((1,), (1,)), ((), ()))


def _cparams(sem):
    return pltpu.CompilerParams(dimension_semantics=sem, vmem_limit_bytes=VMEM_LIMIT)


def _proj_kernel(h_ref, g_ref, w_ref, c_ref, s1_ref, s2_ref, gain_ref, bd_ref, o_ref, *,
                 n_rope, chunk):
    x = h_ref[...]
    ms = jnp.mean(x * x, axis=-1, keepdims=True)
    hn = (x * lax.rsqrt(ms + EPS) * g_ref[...]).astype(BF16)
    cos = c_ref[...]
    sin_lo = s1_ref[...]
    sin_hi = s2_ref[...]
    bd = bd_ref[...]
    d_out = o_ref.shape[-1]
    per = chunk // LANES
    for c0 in range(d_out // chunk):
        acc = jnp.dot(hn, w_ref[:, c0 * chunk:(c0 + 1) * chunk], preferred_element_type=F32)
        for j in range(per):
            blk = c0 * per + j
            xc = acc[:, j * LANES:(j + 1) * LANES]
            if blk < n_rope:
                msq = jnp.dot((xc * xc).astype(BF16), bd, preferred_element_type=F32)
                y = xc * lax.rsqrt(msq + EPS) * gain_ref[:, blk * LANES:(blk + 1) * LANES]
                xc = (y * cos + pltpu.roll(y, 96, 1) * sin_lo + pltpu.roll(y, 32, 1) * sin_hi)
            o_ref[:, blk * LANES:(blk + 1) * LANES] = xc


def _proj(h2d, norm_g, w, tabs, gains, n_rope, tm, seq_tiles):
    n, _ = h2d.shape
    d_out = w.shape[1]
    chunk = 512 if d_out % 512 == 0 else 256
    cos, sin_lo, sin_hi = tabs
    gw = max(n_rope, 1) * LANES
    if gains.shape[1] < gw:
        gains = jnp.ones((1, gw), F32)
    bd = ((jnp.arange(LANES)[:, None] // HEAD_DIM == jnp.arange(LANES)[None, :] // HEAD_DIM)
          .astype(F32) / HEAD_DIM).astype(BF16)
    tab_spec = pl.BlockSpec((tm, LANES), lambda i: (i % seq_tiles, 0))
    return pl.pallas_call(
        functools.partial(_proj_kernel, n_rope=n_rope, chunk=chunk),
        grid=(n // tm,),
        in_specs=[
            pl.BlockSpec((tm, D_MODEL), lambda i: (i, 0)),
            pl.BlockSpec((1, D_MODEL), lambda i: (0, 0)),
            pl.BlockSpec((D_MODEL, d_out), lambda i: (0, 0)),
            tab_spec, tab_spec, tab_spec,
            pl.BlockSpec((1, gw), lambda i: (0, 0)),
            pl.BlockSpec((LANES, LANES), lambda i: (0, 0)),
        ],
        out_specs=pl.BlockSpec((tm, d_out), lambda i: (i, 0)),
        out_shape=jax.ShapeDtypeStruct((n, d_out), F32),
        compiler_params=_cparams(("arbitrary",)),
        name="proj",
    )(h2d, norm_g.reshape(1, D_MODEL), w, cos, sin_lo, sin_hi, gains, bd)


def _pair_block(qv, ks, vs, off, max_dist):
    lane = lax.broadcasted_iota(jnp.int32, (1, LANES), 1)
    lo = lane < HEAD_DIM
    nk = ks[0].shape[0]
    qi = lax.broadcasted_iota(jnp.int32, (BLK, 1), 0)
    kj = lax.broadcasted_iota(jnp.int32, (1, nk), 1)
    dist = off + qi - kj
    mask = (dist >= 0) & (dist <= max_dist)
    outs, lses = [], []
    for a in range(2):
        sel = lo if a == 0 else jnp.logical_not(lo)
        qa = jnp.where(sel, qv, 0.0).astype(BF16)
        s = lax.dot_general(qa, ks[a].astype(BF16), _NT, preferred_element_type=F32)
        s = jnp.where(mask, s, NEG)
        m = jnp.max(s, axis=1, keepdims=True)
        e = jnp.exp(s - m)
        den = jnp.sum(e, axis=1, keepdims=True)
        pv = jnp.dot(e.astype(BF16), vs[a].astype(BF16), preferred_element_type=F32)
        outs.append(pv / den)
        lses.append(m + jnp.log(den))
    o = jnp.where(lo, outs[0], outs[1])
    lse = jnp.where(lo, lses[0], lses[1])
    return o, lse


def _a_attn_kernel(q0, q1, q2, k0, k1, k2, v0, v1, v2, o_ref, o_scr, l_scr):
    seq = q0.shape[1]

    def g0_body(i, carry):
        qs = pl.multiple_of(i * BLK, BLK)
        ks_ = pl.multiple_of(jnp.maximum(i - 1, 0) * BLK, BLK)
        qv = q0[0, pl.ds(qs, BLK), :]
        kv = k0[0, pl.ds(ks_, 2 * BLK), :]
        vv = v0[0, pl.ds(ks_, 2 * BLK), :]
        o, l = _pair_block(qv, (kv, kv), (vv, vv), qs - ks_, BLK)
        o_scr[0, pl.ds(qs, BLK), :] = o
        l_scr[0, pl.ds(qs, BLK), :] = l
        return carry
    lax.fori_loop(0, seq // BLK, g0_body, 0)

    d1 = A_DILS[1]
    nb1 = seq // d1 // BLK
    for r in range(d1):
        def g1_body(i, carry, r=r):
            qs = pl.multiple_of(i * (BLK * d1), BLK * d1)
            kb = jnp.maximum(i - 1, 0)
            ks_ = pl.multiple_of(kb * (BLK * d1), BLK * d1)
            qv = q1[0, pl.ds(qs + r, BLK, stride=d1), :]
            kv = k1[0, pl.ds(ks_ + r, 2 * BLK, stride=d1), :]
            vv = v1[0, pl.ds(ks_ + r, 2 * BLK, stride=d1), :]
            o, l = _pair_block(qv, (kv, kv), (vv, vv), (i - kb) * BLK, BLK)
            o_scr[1, pl.ds(qs + r, BLK, stride=d1), :] = o
            l_scr[1, pl.ds(qs + r, BLK, stride=d1), :] = l
            return carry
        lax.fori_loop(0, nb1, g1_body, 0)

    d2 = A_DILS[2]
    for r in range(d2):
        qv = q2[0, pl.ds(r, BLK, stride=d2), :]
        kv = k2[0, pl.ds(r, BLK, stride=d2), :]
        vv = v2[0, pl.ds(r, BLK, stride=d2), :]
        o, l = _pair_block(qv, (kv, kv), (vv, vv), 0, BLK)
        o_scr[2, pl.ds(r, BLK, stride=d2), :] = o
        l_scr[2, pl.ds(r, BLK, stride=d2), :] = l

    rows = 256

    def merge_body(i, carry):
        rs = pl.multiple_of(i * rows, rows)
        la = l_scr[0, pl.ds(rs, rows), :]
        lb = l_scr[1, pl.ds(rs, rows), :]
        lc = l_scr[2, pl.ds(rs, rows), :]
        m = jnp.maximum(jnp.maximum(la, lb), lc)
        ea = jnp.exp(la - m)
        eb = jnp.exp(lb - m)
        ec = jnp.exp(lc - m)
        tot = ea + eb + ec
        o = (ea / tot) * o_scr[0, pl.ds(rs, rows), :]
        o = o + (eb / tot) * o_scr[1, pl.ds(rs, rows), :]
        o = o + (ec / tot) * o_scr[2, pl.ds(rs, rows), :]
        o_ref[0, pl.ds(rs, rows), :] = o
        return carry
    lax.fori_loop(0, seq // rows, merge_body, 0)


def _a_attn_prompt(qkv):
    b, seq, _ = qkv.shape
    pairs = A_WIDTH // LANES
    nq = N_GROUPS * pairs

    def spec(base, g):
        return pl.BlockSpec((1, seq, LANES), lambda bi, p, base=base, g=g: (bi, 0, base + g * pairs + p))
    in_specs = ([spec(0, g) for g in range(N_GROUPS)] + [spec(nq, g) for g in range(N_GROUPS)]
                + [spec(2 * nq, g) for g in range(N_GROUPS)])
    return pl.pallas_call(
        _a_attn_kernel,
        grid=(b, pairs),
        in_specs=in_specs,
        out_specs=pl.BlockSpec((1, seq, LANES), lambda bi, p: (bi, 0, p)),
        out_shape=jax.ShapeDtypeStruct((b, seq, A_WIDTH), F32),
        scratch_shapes=[pltpu.VMEM((N_GROUPS, seq, LANES), F32),
                        pltpu.VMEM((N_GROUPS, seq, LANES), F32)],
        compiler_params=_cparams(("arbitrary", "arbitrary")),
        name="a_attn_prompt",
    )(*([qkv] * 9))


def _b_attn_kernel(sink_ref, q_ref, k_ref, v_ref, o_ref, ksw_scr, vsw_scr):
    seq = q_ref.shape[1]
    p = pl.program_id(1)
    c = p // (B_HEADS // B_KV_HEADS // 2)
    lane = lax.broadcasted_iota(jnp.int32, (1, LANES), 1)
    lo = lane < HEAD_DIM
    ksw_scr[...] = pltpu.roll(k_ref[0], HEAD_DIM, 1)
    vsw_scr[...] = pltpu.roll(v_ref[0], HEAD_DIM, 1)
    s0 = sink_ref[0, 2 * p]
    s1 = sink_ref[0, 2 * p + 1]
    sink = jnp.where(lo, s0, s1)

    def body(i, carry):
        qs = pl.multiple_of(i * BLK, BLK)
        ks_ = pl.multiple_of(jnp.maximum(i - 1, 0) * BLK, BLK)
        qv = q_ref[0, pl.ds(qs, BLK), :]
        kn = k_ref[0, pl.ds(ks_, 2 * BLK), :]
        vn = v_ref[0, pl.ds(ks_, 2 * BLK), :]
        kw = ksw_scr[pl.ds(ks_, 2 * BLK), :]
        vw = vsw_scr[pl.ds(ks_, 2 * BLK), :]
        is0 = c == 0
        k_a0 = jnp.where(is0, kn, kw)
        k_a1 = jnp.where(is0, kw, kn)
        v_a0 = jnp.where(is0, vn, vw)
        v_a1 = jnp.where(is0, vw, vn)
        o, l = _pair_block(qv, (k_a0, k_a1), (v_a0, v_a1), qs - ks_, B_WINDOW - 1)
        gate = 1.0 / (1.0 + jnp.exp(sink - l))
        o_ref[0, pl.ds(qs, BLK), :] = o * gate
        return carry
    lax.fori_loop(0, seq // BLK, body, 0)


def _b_attn_prompt(q, kv, sinks):
    b, seq, _ = q.shape
    pairs = B_HEADS * HEAD_DIM // LANES
    return pl.pallas_call(
        _b_attn_kernel,
        grid=(b, pairs),
        in_specs=[
            pl.BlockSpec(memory_space=pltpu.SMEM),
            pl.BlockSpec((1, seq, LANES), lambda bi, p: (bi, 0, p)),
            pl.BlockSpec((1, seq, LANES), lambda bi, p: (bi, 0, 0)),
            pl.BlockSpec((1, seq, LANES), lambda bi, p: (bi, 0, 1)),
        ],
        out_specs=pl.BlockSpec((1, seq, LANES), lambda bi, p: (bi, 0, p)),
        out_shape=jax.ShapeDtypeStruct((b, seq, B_HEADS * HEAD_DIM), F32),
        scratch_shapes=[pltpu.VMEM((seq, LANES), F32), pltpu.VMEM((seq, LANES), F32)],
        compiler_params=_cparams(("arbitrary", "arbitrary")),
        name="b_attn_prompt",
    )(sinks.reshape(1, B_HEADS), q, kv, kv)


FF_CHUNK = 256


def _silu_gate(cg, cv):
    return (cg / (1.0 + jnp.exp(-cg))) * cv


def _ffn_prompt_kernel(h_ref, o_ref, wo_ref, g_ref, wup_ref, cw_ref, cb_ref, wdn_ref,
                       hout_ref, tail_ref):
    i = pl.program_id(1)
    tm = h_ref.shape[1]

    @pl.when(i == 0)
    def _():
        tail_ref[...] = jnp.zeros_like(tail_ref)

    h2 = h_ref[0] + jnp.dot(o_ref[0].astype(BF16), wo_ref[...], preferred_element_type=F32)
    ms = jnp.mean(h2 * h2, axis=-1, keepdims=True)
    hn = (h2 * lax.rsqrt(ms + EPS) * g_ref[...]).astype(BF16)
    row = lax.broadcasted_iota(jnp.int32, (tm, 1), 0)
    acc = jnp.zeros((tm, D_MODEL), F32)
    for j in range(D_FF // FF_CHUNK):
        parts = []
        for base in (0, D_FF):
            c0 = base + j * FF_CHUNK
            sl = slice(c0, c0 + FF_CHUNK)
            u = jnp.dot(hn, wup_ref[:, sl], preferred_element_type=F32)
            prev = tail_ref[0, :, sl]
            p1 = prev[7:8, :]
            p2 = prev[6:7, :]
            u1 = jnp.where(row == 0, p1, pltpu.roll(u, 1, 0))
            u2 = jnp.where(row == 0, p2, jnp.where(row == 1, p1, pltpu.roll(u, 2, 0)))
            cwj = cw_ref[:, sl]
            parts.append(cwj[0:1, :] * u2 + cwj[1:2, :] * u1 + cwj[2:3, :] * u + cb_ref[:, sl])
            tail_ref[0, :, sl] = u[tm - 8:tm, :]
        act = _silu_gate(parts[0], parts[1]).astype(BF16)
        acc = acc + jnp.dot(act, wdn_ref[j * FF_CHUNK:(j + 1) * FF_CHUNK, :],
                            preferred_element_type=F32)
    hout_ref[0] = h2 + acc


def _ffn_prompt(h, o, w_out, g, w_up, conv_w, conv_b, w_down, tm=512):
    b, seq, _ = h.shape
    ko = o.shape[-1]
    const = lambda bi, i: (0, 0)
    once = pl.Buffered(1)
    return pl.pallas_call(
        _ffn_prompt_kernel,
        grid=(b, seq // tm),
        in_specs=[
            pl.BlockSpec((1, tm, D_MODEL), lambda bi, i: (bi, i, 0)),
            pl.BlockSpec((1, tm, ko), lambda bi, i: (bi, i, 0)),
            pl.BlockSpec((ko, D_MODEL), const, pipeline_mode=once),
            pl.BlockSpec((1, D_MODEL), const),
            pl.BlockSpec((D_MODEL, 2 * D_FF), const, pipeline_mode=once),
            pl.BlockSpec((3, 2 * D_FF), const),
            pl.BlockSpec((1, 2 * D_FF), const),
            pl.BlockSpec((D_FF, D_MODEL), const, pipeline_mode=once),
        ],
        out_specs=[
            pl.BlockSpec((1, tm, D_MODEL), lambda bi, i: (bi, i, 0)),
            pl.BlockSpec((1, 8, 2 * D_FF), lambda bi, i: (bi, 0, 0)),
        ],
        out_shape=[jax.ShapeDtypeStruct((b, seq, D_MODEL), F32),
                   jax.ShapeDtypeStruct((b, 8, 2 * D_FF), F32)],
        compiler_params=_cparams(("arbitrary", "arbitrary")),
        name="ffn_prompt",
    )(h, o, w_out, g.reshape(1, D_MODEL), w_up, conv_w, conv_b.reshape(1, 2 * D_FF), w_down)


def _ffn_sample_kernel(h_ref, ot_ref, lt_ref, wo_ref, g_ref, wup_ref, cw_ref, cb_ref, wdn_ref,
                       st0_ref, st1_ref, hout_ref, u_ref, *, n_groups):
    if n_groups == 1:
        ot = ot_ref[0]
    else:
        m = lt_ref[0]
        for gi in range(1, n_groups):
            m = jnp.maximum(m, lt_ref[gi])
        es = [jnp.exp(lt_ref[gi] - m) for gi in range(n_groups)]
        tot = es[0]
        for gi in range(1, n_groups):
            tot = tot + es[gi]
        ot = (es[0] / tot) * ot_ref[0]
        for gi in range(1, n_groups):
            ot = ot + (es[gi] / tot) * ot_ref[gi]
    o = ot.T.astype(BF16)
    h2 = h_ref[...] + jnp.dot(o, wo_ref[...], preferred_element_type=F32)
    ms = jnp.mean(h2 * h2, axis=-1, keepdims=True)
    hn = (h2 * lax.rsqrt(ms + EPS) * g_ref[...]).astype(BF16)
    acc = jnp.zeros(h2.shape, F32)
    for j in range(D_FF // FF_CHUNK):
        parts = []
        for base in (0, D_FF):
            c0 = base + j * FF_CHUNK
            sl = slice(c0, c0 + FF_CHUNK)
            u = jnp.dot(hn, wup_ref[:, sl], preferred_element_type=F32)
            u_ref[:, sl] = u
            cwj = cw_ref[:, sl]
            parts.append(cwj[0:1, :] * st0_ref[:, sl] + cwj[1:2, :] * st1_ref[:, sl]
                         + cwj[2:3, :] * u + cb_ref[:, sl])
        act = _silu_gate(parts[0], parts[1]).astype(BF16)
        acc = acc + jnp.dot(act, wdn_ref[j * FF_CHUNK:(j + 1) * FF_CHUNK, :],
                            preferred_element_type=F32)
    hout_ref[...] = h2 + acc


def _ffn_sample(h, ot, lt, w_out, g, w_up, conv_w, conv_b, w_down, st0, st1):
    n = h.shape[0]
    n_groups = ot.shape[0]
    full = lambda a: pl.BlockSpec(a.shape, lambda i, nd=a.ndim: (0,) * nd)
    args = (h, ot, lt, w_out, g.reshape(1, D_MODEL), w_up, conv_w, conv_b.reshape(1, 2 * D_FF),
            w_down, st0, st1)
    return pl.pallas_call(
        functools.partial(_ffn_sample_kernel, n_groups=n_groups),
        grid=(1,),
        in_specs=[full(a) for a in args],
        out_specs=[pl.BlockSpec((n, D_MODEL), lambda i: (0, 0)),
                   pl.BlockSpec((n, 2 * D_FF), lambda i: (0, 0))],
        out_shape=[jax.ShapeDtypeStruct((n, D_MODEL), F32),
                   jax.ShapeDtypeStruct((n, 2 * D_FF), F32)],
        compiler_params=_cparams(("arbitrary",)),
        name="ffn_sample",
    )(*args)


def _sample_attn_kernel(*refs, hb, rep, dil, wmin, has_new, shift, use_sink, aliased):
    refs = list(refs)
    sink_ref = refs.pop(0) if use_sink else None
    qt_ref, kt_ref, vt_ref, cache_ref = refs[:4]
    refs = refs[4:]
    if aliased:
        refs = refs[1:]
    ot_ref, lt_ref = refs[:2]
    newc_ref = refs[2] if shift else None

    n = pl.program_id(0)
    hc = pl.program_id(1)
    w = cache_ref.shape[-1]
    lane = lax.broadcasted_iota(jnp.int32, (1, LANES), 1)
    sel = lane == n
    lanew = lax.broadcasted_iota(jnp.int32, (1, w), 1)
    valid = ((lanew & (dil - 1)) == 0) & (lanew >= wmin)
    last = lanew == (w - 1)

    @pl.when((n == 0) & (hc == 0))
    def _():
        ot_ref[...] = jnp.zeros_like(ot_ref)
        lt_ref[...] = jnp.zeros_like(lt_ref)

    def column(ref, row0):
        return jnp.sum(jnp.where(sel, ref[pl.ds(row0, HEAD_DIM), :], 0.0), axis=1, keepdims=True)

    for j in range(hb):
        kvh = hc * hb + j
        kt = cache_ref[0, 0, 0, j]
        vt = cache_ref[0, 0, 1, j]
        if has_new:
            krow = pl.multiple_of(kvh * HEAD_DIM, HEAD_DIM)
            kcol = column(kt_ref, krow)
            vcol = column(vt_ref, krow)
        for t in range(rep):
            head = kvh * rep + t
            row0 = pl.multiple_of(head * HEAD_DIM, HEAD_DIM)
            qcol = column(qt_ref, row0)
            s = jnp.sum(kt * qcol, axis=0, keepdims=True)
            s = jnp.where(valid, s, NEG)
            m = jnp.max(s, axis=1, keepdims=True)
            if has_new:
                s_new = jnp.sum(qcol * kcol, axis=0, keepdims=True)
                m = jnp.maximum(m, s_new)
            e = jnp.exp(s - m)
            den = jnp.sum(e, axis=1, keepdims=True)
            o = jnp.sum(vt * e, axis=1, keepdims=True)
            if has_new:
                e_new = jnp.exp(s_new - m)
                den = den + e_new
                o = o + e_new * vcol
            o = o / den
            lse = m + jnp.log(den)
            if use_sink:
                o = o * (1.0 / (1.0 + jnp.exp(sink_ref[0, head] - lse)))
            ot_ref[pl.ds(row0, HEAD_DIM), :] = jnp.where(sel, o, ot_ref[pl.ds(row0, HEAD_DIM), :])
            lt_ref[pl.ds(row0, HEAD_DIM), :] = jnp.where(sel, lse, lt_ref[pl.ds(row0, HEAD_DIM), :])
        if shift:
            newc_ref[0, 0, 0, j] = jnp.where(last, kcol, pltpu.roll(kt, w - 1, 1))
            newc_ref[0, 0, 1, j] = jnp.where(last, vcol, pltpu.roll(vt, w - 1, 1))


def _sample_attn(qt, kt, vt, cache, layer, *, rep, dil, wmin, has_new, shift, sinks=None,
                 prev_out=None):
    nl, n, _, hkv, _, w = cache.shape
    hq = hkv * rep
    hb = hkv if w * hkv <= 8192 else 4
    use_sink = sinks is not None
    aliased = prev_out is not None
    const2 = lambda ni, hc: (0, 0)
    cache_spec = pl.BlockSpec((1, 1, 2, hb, HEAD_DIM, w), lambda ni, hc: (layer, ni, 0, hc, 0, 0))
    in_specs, args = [], []
    if use_sink:
        in_specs.append(pl.BlockSpec(memory_space=pltpu.SMEM))
        args.append(sinks.reshape(1, hq))
    in_specs += [pl.BlockSpec((hq * HEAD_DIM, n), const2), pl.BlockSpec((hkv * HEAD_DIM, n), const2),
                 pl.BlockSpec((hkv * HEAD_DIM, n), const2), cache_spec]
    args += [qt, kt, vt, cache]
    io_alias = {}
    if aliased:
        in_specs.append(pl.BlockSpec(memory_space=pl.ANY))
        io_alias = {len(args): 2}
        args.append(prev_out)
    out_specs = [pl.BlockSpec((hq * HEAD_DIM, n), const2), pl.BlockSpec((hq * HEAD_DIM, n), const2)]
    out_shape = [jax.ShapeDtypeStruct((hq * HEAD_DIM, n), F32)] * 2
    if shift:
        out_specs.append(cache_spec)
        out_shape.append(jax.ShapeDtypeStruct(cache.shape, F32))
    return pl.pallas_call(
        functools.partial(_sample_attn_kernel, hb=hb, rep=rep, dil=dil, wmin=wmin, has_new=has_new,
                          shift=shift, use_sink=use_sink, aliased=aliased),
        grid=(n, hkv // hb),
        in_specs=in_specs,
        out_specs=out_specs,
        out_shape=out_shape,
        input_output_aliases=io_alias,
        compiler_params=_cparams(("arbitrary", "arbitrary")),
        name="sample_attn",
    )(*args)


def _rope_tables(pos):
    half = HEAD_DIM // 2
    inv_freq = ROPE_THETA ** (-jnp.arange(half, dtype=F32) / half)
    ang = pos.astype(F32)[:, None] * inv_freq[None, :]
    cos, sin = jnp.cos(ang), jnp.sin(ang)
    idx = jnp.arange(LANES) % half
    first = (jnp.arange(LANES) % HEAD_DIM) < half
    c = cos[:, idx]
    s = sin[:, idx]
    return c, jnp.where(first[None, :], -s, 0.0), jnp.where(first[None, :], 0.0, s)


def _tile_gain(g, heads):
    return jnp.tile(g.astype(F32), heads)


def kernel(x_prompt, x_sample, cache_a_kv_w128, cache_a_kv_w512, cache_a_kv_w2048, cache_b_kv,
           state_ffn_conv, attn_norm_g, ffn_norm_g, a_w_in, a_q_norm_g, a_k_norm_g, a_w_out,
           b_kv_norm_g, b_w_kv, b_k_norm_g, b_w_q, b_q_norm_g, b_sinks, b_w_out,
           ffn_w_up, ffn_conv_w, ffn_conv_b, ffn_w_down):
    batch, seq, _ = x_prompt.shape
    n_s = x_sample.shape[0]
    depth = attn_norm_g.shape[0]
    n_a = a_w_in.shape[0]
    scale = HEAD_DIM ** -0.5

    a_w_in_b = a_w_in.astype(BF16)
    a_w_out_b = a_w_out.astype(BF16)
    b_w_kv_b = b_w_kv.astype(BF16)
    b_w_q_b = b_w_q.astype(BF16)
    b_w_out_b = b_w_out.astype(BF16)
    w_up_b = ffn_w_up.astype(BF16)
    w_down_b = ffn_w_down.astype(BF16)

    tabs_p = _rope_tables(jnp.arange(seq))
    tabs_s = _rope_tables(jnp.full((n_s,), PAST_LEN))

    def a_gains(layer):
        gq = jnp.concatenate([_tile_gain(a_q_norm_g[layer, g], A_HEADS) for g in range(N_GROUPS)]) * scale
        gk = jnp.concatenate([_tile_gain(a_k_norm_g[layer, g], A_HEADS) for g in range(N_GROUPS)])
        return jnp.concatenate([gq, gk]).reshape(1, -1)

    kv_gain = _tile_gain(b_k_norm_g, B_KV_HEADS).reshape(1, -1)

    caches_t = [jnp.transpose(c, (0, 1, 2, 4, 5, 3))
                for c in (cache_a_kv_w128, cache_a_kv_w512, cache_a_kv_w2048)]
    cache_b_t = jnp.transpose(cache_b_kv, (0, 1, 3, 4, 2))[None]

    hp = x_prompt
    hs = x_sample.reshape(n_s, D_MODEL)
    a_rows_p = [[] for _ in range(N_GROUPS)]
    new_caches = [None] * N_GROUPS
    conv_p, conv_s = [], []
    kv_p = None
    new_b = None
    kv_st = None
    n_q = N_GROUPS * A_WIDTH

    for layer in range(depth):
        if layer < n_a:
            gains = a_gains(layer)
            qkv = _proj(hp.reshape(batch * seq, D_MODEL), attn_norm_g[layer], a_w_in_b[layer], tabs_p,
                        gains, 2 * n_q // LANES, 256, seq // 256).reshape(batch, seq, 3 * n_q)
            for g, win in enumerate(A_WINDOWS):
                keep = min(win, seq)
                kk = qkv[:, seq - keep:, n_q + g * A_WIDTH:n_q + (g + 1) * A_WIDTH]
                vv = qkv[:, seq - keep:, 2 * n_q + g * A_WIDTH:2 * n_q + (g + 1) * A_WIDTH]
                a_rows_p[g].append(jnp.stack([kk, vv], axis=1).reshape(batch, 2, keep, A_HEADS, HEAD_DIM))
            o_p = _a_attn_prompt(qkv)
            w_out_l = a_w_out_b[layer]
            qkv_s = _proj(hs, attn_norm_g[layer], a_w_in_b[layer], tabs_s, gains, 2 * n_q // LANES,
                          n_s, 1)
            qkv_st = qkv_s.T
            ots, lts = [], []
            for g in range(N_GROUPS):
                qt = qkv_st[g * A_WIDTH:(g + 1) * A_WIDTH]
                kt = qkv_st[n_q + g * A_WIDTH:n_q + (g + 1) * A_WIDTH]
                vt = qkv_st[2 * n_q + g * A_WIDTH:2 * n_q + (g + 1) * A_WIDTH]
                ot, lt, new_caches[g] = _sample_attn(
                    qt, kt, vt, caches_t[g], layer, rep=1, dil=A_DILS[g], wmin=0, has_new=True,
                    shift=True, prev_out=new_caches[g])
                ots.append(ot)
                lts.append(lt)
            ot_s = jnp.stack(ots)
            lt_s = jnp.stack(lts)
        else:
            j = layer - n_a
            q_gain = (_tile_gain(b_q_norm_g[j], B_HEADS) * scale).reshape(1, -1)
            rep_b = B_HEADS // B_KV_HEADS
            q_s = _proj(hs, attn_norm_g[layer], b_w_q_b[j], tabs_s, q_gain, B_HEADS * HEAD_DIM // LANES,
                        n_s, 1)
            if layer == n_a:
                kv_p = _proj(hp.reshape(batch * seq, D_MODEL), b_kv_norm_g, b_w_kv_b, tabs_p, kv_gain,
                             1, 256, seq // 256).reshape(batch, seq, 2 * LANES)
                kv_st = _proj(hs, b_kv_norm_g, b_w_kv_b, tabs_s, kv_gain, 1, n_s, 1).T
                ot, lt, new_b = _sample_attn(q_s.T, kv_st[:LANES], kv_st[LANES:], cache_b_t, 0, rep=rep_b,
                                             dil=1, wmin=1, has_new=True, shift=True, sinks=b_sinks[j])
            else:
                ot, lt = _sample_attn(q_s.T, kv_st[:LANES], kv_st[LANES:], new_b, 0, rep=rep_b, dil=1,
                                      wmin=0, has_new=False, shift=False, sinks=b_sinks[j])
            q_p = _proj(hp.reshape(batch * seq, D_MODEL), attn_norm_g[layer], b_w_q_b[j], tabs_p, q_gain,
                        B_HEADS * HEAD_DIM // LANES, 256, seq // 256).reshape(batch, seq, -1)
            o_p = _b_attn_prompt(q_p, kv_p, b_sinks[j])
            w_out_l = b_w_out_b[j]
            ot_s = ot[None]
            lt_s = lt[None]

        hp, tail = _ffn_prompt(hp, o_p, w_out_l, ffn_norm_g[layer], w_up_b[layer], ffn_conv_w[layer],
                               ffn_conv_b[layer], w_down_b[layer])
        conv_p.append(tail[:, 6:8, :])
        st0 = state_ffn_conv[layer, :, 0, :]
        st1 = state_ffn_conv[layer, :, 1, :]
        hs, u_s = _ffn_sample(hs, ot_s, lt_s, w_out_l, ffn_norm_g[layer], w_up_b[layer],
                              ffn_conv_w[layer], ffn_conv_b[layer], w_down_b[layer], st0, st1)
        conv_s.append(jnp.stack([st1, u_s], axis=1))

    a_out_p = [jnp.stack(r, axis=0) for r in a_rows_p]
    a_out_s = [jnp.transpose(c, (0, 1, 2, 5, 3, 4)) for c in new_caches]
    b_p = jnp.stack([kv_p[:, seq - B_WINDOW:, :LANES], kv_p[:, seq - B_WINDOW:, LANES:]], axis=1)
    b_p = b_p.reshape(batch, 2, B_WINDOW, B_KV_HEADS, HEAD_DIM)
    b_s = jnp.transpose(new_b[0], (0, 1, 4, 2, 3))
    return (hp, hs.reshape(n_s, 1, D_MODEL), a_out_p[0], a_out_s[0], a_out_p[1], a_out_s[1],
            a_out_p[2], a_out_s[2], b_p, b_s, jnp.stack(conv_p, axis=0), jnp.stack(conv_s, axis=0))
```

```python
import functools

import jax
import jax.numpy as jnp
from jax import lax
from jax.experimental import pallas as pl
from jax.experimental.pallas import tpu as pltpu

D_MODEL = 1024
HEAD_DIM = 64
N_GROUPS = 3
A_DILS = (1, 4, 16)
A_WINDOWS = (128, 512, 2048)
A_HEADS = 8
A_WIDTH = A_HEADS * HEAD_DIM
B_HEADS = 16
B_KV_HEADS = 2
B_WINDOW = 128
D_FF = 2816
ROPE_THETA = 10000.0
EPS = 1e-6
PAST_LEN = 8192
LANES = 128
BLK = 128
NEG = -1e30
VMEM_LIMIT = 56 * 1024 * 1024
SAMPLE_BLOCK_BYTES = 4 * 1024 * 1024

F32 = jnp.float32
BF16 = jnp.bfloat16
_NT = (((1,), (1,)), ((), ()))


def _cparams(sem):
    return pltpu.CompilerParams(dimension_semantics=sem, vmem_limit_bytes=VMEM_LIMIT)


def _proj_kernel(h_ref, g_ref, w_ref, c_ref, s1_ref, s2_ref, gain_ref, bd_ref, o_ref, *,
                 n_rope, chunk):
    x = h_ref[...]
    ms = jnp.mean(x * x, axis=-1, keepdims=True)
    hn = (x * lax.rsqrt(ms + EPS) * g_ref[...]).astype(BF16)
    cos = c_ref[...]
    sin_lo = s1_ref[...]
    sin_hi = s2_ref[...]
    bd = bd_ref[...]
    d_out = o_ref.shape[-1]
    per = chunk // LANES
    for c0 in range(d_out // chunk):
        acc = jnp.dot(hn, w_ref[:, c0 * chunk:(c0 + 1) * chunk], preferred_element_type=F32)
        for j in range(per):
            blk = c0 * per + j
            xc = acc[:, j * LANES:(j + 1) * LANES]
            if blk < n_rope:
                msq = jnp.dot((xc * xc).astype(BF16), bd, preferred_element_type=F32)
                y = xc * lax.rsqrt(msq + EPS) * gain_ref[:, blk * LANES:(blk + 1) * LANES]
                xc = (y * cos + pltpu.roll(y, 96, 1) * sin_lo + pltpu.roll(y, 32, 1) * sin_hi)
            o_ref[:, blk * LANES:(blk + 1) * LANES] = xc


def _proj(h2d, norm_g, w, tabs, gains, n_rope, tm, seq_tiles):
    n, _ = h2d.shape
    d_out = w.shape[1]
    chunk = 512 if d_out % 512 == 0 else 256
    cos, sin_lo, sin_hi = tabs
    gw = max(n_rope, 1) * LANES
    if gains.shape[1] < gw:
        gains = jnp.ones((1, gw), F32)
    bd = ((jnp.arange(LANES)[:, None] // HEAD_DIM == jnp.arange(LANES)[None, :] // HEAD_DIM)
          .astype(F32) / HEAD_DIM).astype(BF16)
    tab_spec = pl.BlockSpec((tm, LANES), lambda i: (i % seq_tiles, 0))
    return pl.pallas_call(
        functools.partial(_proj_kernel, n_rope=n_rope, chunk=chunk),
        grid=(n // tm,),
        in_specs=[
            pl.BlockSpec((tm, D_MODEL), lambda i: (i, 0)),
            pl.BlockSpec((1, D_MODEL), lambda i: (0, 0)),
            pl.BlockSpec((D_MODEL, d_out), lambda i: (0, 0)),
            tab_spec, tab_spec, tab_spec,
            pl.BlockSpec((1, gw), lambda i: (0, 0)),
            pl.BlockSpec((LANES, LANES), lambda i: (0, 0)),
        ],
        out_specs=pl.BlockSpec((tm, d_out), lambda i: (i, 0)),
        out_shape=jax.ShapeDtypeStruct((n, d_out), F32),
        compiler_params=_cparams(("arbitrary",)),
        name="proj",
    )(h2d, norm_g.reshape(1, D_MODEL), w, cos, sin_lo, sin_hi, gains, bd)


NB = 8


def _band_bias(off, nk, max_dist):
    qi = lax.broadcasted_iota(jnp.int32, (BLK, 1), 0)
    kj = lax.broadcasted_iota(jnp.int32, (1, nk), 1)
    dist = off + qi - kj
    return jnp.where((dist >= 0) & (dist <= max_dist), 0.0, NEG).astype(F32)


def _init_bias(bias_scr, max_dist):
    bias_scr[0] = _band_bias(0, 2 * BLK, max_dist)
    bias_scr[1] = _band_bias(BLK, 2 * BLK, max_dist)
    qi = lax.broadcasted_iota(jnp.int32, (BLK, 1), 0)
    kj = lax.broadcasted_iota(jnp.int32, (1, 2 * BLK), 1)
    bias_scr[2] = jnp.where((kj >= BLK) & (kj - BLK <= qi), 0.0, NEG).astype(F32)


def _attend(blocks):
    lane = lax.broadcasted_iota(jnp.int32, (1, LANES), 1)
    lo = lane < HEAD_DIM
    hi = jnp.logical_not(lo)
    scores = []
    for q, k, _, _ in blocks:
        for sel in (lo, hi):
            qa = jnp.where(sel, q, 0.0).astype(BF16)
            scores.append(lax.dot_general(qa, k, _NT, preferred_element_type=F32))
    parts = []
    for bi, (_, _, v, bias) in enumerate(blocks):
        for a in range(2):
            s = scores[2 * bi + a] + bias
            m = jnp.max(s, axis=1, keepdims=True)
            e = jnp.exp(s - m)
            den = jnp.sum(e, axis=1, keepdims=True)
            pv = jnp.dot(e.astype(BF16), v, preferred_element_type=F32)
            parts.append((pv, den, m))
    outs = []
    for bi in range(len(blocks)):
        (pv0, d0, m0), (pv1, d1, m1) = parts[2 * bi], parts[2 * bi + 1]
        den = jnp.where(lo, d0, d1)
        o = jnp.where(lo, pv0, pv1) / den
        lse = jnp.where(lo, m0, m1) + jnp.log(den)
        outs.append((o, lse))
    return outs


def _a_attn_kernel(q0, q1, q2, k0, k1, k2, v0, v1, v2, o_ref, o_scr, l_scr, bias_scr):
    seq = q0.shape[1]

    @pl.when((pl.program_id(0) == 0) & (pl.program_id(1) == 0))
    def _():
        _init_bias(bias_scr, BLK)

    def g0_body(it, carry):
        blocks, starts = [], []
        for j in range(NB):
            i = it * NB + j
            qs = pl.multiple_of(i * BLK, BLK)
            ks_ = pl.multiple_of(jnp.maximum(i - 1, 0) * BLK, BLK)
            if j == 0:
                bias = jnp.where(it == 0, bias_scr[0], bias_scr[1])
            else:
                bias = bias_scr[1]
            blocks.append((q0[0, pl.ds(qs, BLK), :], k0[0, pl.ds(ks_, 2 * BLK), :].astype(BF16),
                           v0[0, pl.ds(ks_, 2 * BLK), :].astype(BF16), bias))
            starts.append(qs)
        for qs, (o, l) in zip(starts, _attend(blocks)):
            o_scr[0, pl.ds(qs, BLK), :] = o
            l_scr[0, pl.ds(qs, BLK), :] = l
        return carry
    lax.fori_loop(0, seq // BLK // NB, g0_body, 0)

    d1 = A_DILS[1]
    nb1 = seq // d1 // BLK

    spi = NB // nb1

    def g1_body(it, carry):
        blocks, starts = [], []
        for j in range(spi):
            r = it * spi + j
            for i in range(nb1):
                kb = max(i - 1, 0)
                blocks.append((q1[0, pl.ds(i * BLK * d1 + r, BLK, stride=d1), :],
                               k1[0, pl.ds(kb * BLK * d1 + r, 2 * BLK, stride=d1), :].astype(BF16),
                               v1[0, pl.ds(kb * BLK * d1 + r, 2 * BLK, stride=d1), :].astype(BF16),
                               bias_scr[min(i, 1)]))
                starts.append(i * BLK * d1 + r)
        for qs, (o, l) in zip(starts, _attend(blocks)):
            o_scr[1, pl.ds(qs, BLK, stride=d1), :] = o
            l_scr[1, pl.ds(qs, BLK, stride=d1), :] = l
        return carry
    lax.fori_loop(0, d1 // spi, g1_body, 0)

    d2 = A_DILS[2]

    def g2_body(it, carry):
        blocks = []
        for j in range(0, NB, 2):
            ra = it * NB + j
            rb = ra + 1
            kk = jnp.concatenate([k2[0, pl.ds(ra, BLK, stride=d2), :],
                                  k2[0, pl.ds(rb, BLK, stride=d2), :]], axis=0).astype(BF16)
            vv = jnp.concatenate([v2[0, pl.ds(ra, BLK, stride=d2), :],
                                  v2[0, pl.ds(rb, BLK, stride=d2), :]], axis=0).astype(BF16)
            blocks.append((q2[0, pl.ds(ra, BLK, stride=d2), :], kk, vv, bias_scr[0]))
            blocks.append((q2[0, pl.ds(rb, BLK, stride=d2), :], kk, vv, bias_scr[2]))
        for j, (o, l) in enumerate(_attend(blocks)):
            r = it * NB + j
            o_scr[2, pl.ds(r, BLK, stride=d2), :] = o
            l_scr[2, pl.ds(r, BLK, stride=d2), :] = l
        return carry
    lax.fori_loop(0, d2 // NB, g2_body, 0)

    rows = 256

    def merge_body(i, carry):
        rs = pl.multiple_of(i * rows, rows)
        la = l_scr[0, pl.ds(rs, rows), :]
        lb = l_scr[1, pl.ds(rs, rows), :]
        lc = l_scr[2, pl.ds(rs, rows), :]
        m = jnp.maximum(jnp.maximum(la, lb), lc)
        ea = jnp.exp(la - m)
        eb = jnp.exp(lb - m)
        ec = jnp.exp(lc - m)
        tot = ea + eb + ec
        o = (ea / tot) * o_scr[0, pl.ds(rs, rows), :]
        o = o + (eb / tot) * o_scr[1, pl.ds(rs, rows), :]
        o = o + (ec / tot) * o_scr[2, pl.ds(rs, rows), :]
        o_ref[0, pl.ds(rs, rows), :] = o
        return carry
    lax.fori_loop(0, seq // rows, merge_body, 0)


def _a_attn_prompt(qkv):
    b, seq, _ = qkv.shape
    pairs = A_WIDTH // LANES
    nq = N_GROUPS * pairs

    def spec(base, g):
        return pl.BlockSpec((1, seq, LANES), lambda bi, p, base=base, g=g: (bi, 0, base + g * pairs + p))
    in_specs = ([spec(0, g) for g in range(N_GROUPS)] + [spec(nq, g) for g in range(N_GROUPS)]
                + [spec(2 * nq, g) for g in range(N_GROUPS)])
    return pl.pallas_call(
        _a_attn_kernel,
        grid=(b, pairs),
        in_specs=in_specs,
        out_specs=pl.BlockSpec((1, seq, LANES), lambda bi, p: (bi, 0, p)),
        out_shape=jax.ShapeDtypeStruct((b, seq, A_WIDTH), F32),
        scratch_shapes=[pltpu.VMEM((N_GROUPS, seq, LANES), F32),
                        pltpu.VMEM((N_GROUPS, seq, LANES), F32),
                        pltpu.VMEM((3, BLK, 2 * BLK), F32)],
        compiler_params=_cparams(("arbitrary", "arbitrary")),
        name="a_attn_prompt",
    )(*([qkv] * 9))


def _b_attn_kernel(sink_ref, q_ref, k_ref, v_ref, o_ref, kdup_scr, vdup_scr, bias_scr):
    seq = q_ref.shape[1]
    p = pl.program_id(1)
    c = p // (B_HEADS // B_KV_HEADS // 2)
    lane = lax.broadcasted_iota(jnp.int32, (1, LANES), 1)
    lo = lane < HEAD_DIM

    @pl.when((pl.program_id(0) == 0) & (p == 0))
    def _():
        _init_bias(bias_scr, B_WINDOW - 1)

    keep = (lane >= HEAD_DIM).astype(jnp.int32) == c
    kdup_scr[...] = jnp.where(keep, k_ref[0], pltpu.roll(k_ref[0], HEAD_DIM, 1)).astype(BF16)
    vdup_scr[...] = jnp.where(keep, v_ref[0], pltpu.roll(v_ref[0], HEAD_DIM, 1)).astype(BF16)
    sink = jnp.where(lo, sink_ref[0, 2 * p], sink_ref[0, 2 * p + 1])

    def body(it, carry):
        blocks, starts = [], []
        for j in range(NB):
            i = it * NB + j
            qs = pl.multiple_of(i * BLK, BLK)
            ks_ = pl.multiple_of(jnp.maximum(i - 1, 0) * BLK, BLK)
            if j == 0:
                bias = jnp.where(it == 0, bias_scr[0], bias_scr[1])
            else:
                bias = bias_scr[1]
            blocks.append((q_ref[0, pl.ds(qs, BLK), :], kdup_scr[pl.ds(ks_, 2 * BLK), :],
                           vdup_scr[pl.ds(ks_, 2 * BLK), :], bias))
            starts.append(qs)
        for qs, (o, l) in zip(starts, _attend(blocks)):
            gate = 1.0 / (1.0 + jnp.exp(sink - l))
            o_ref[0, pl.ds(qs, BLK), :] = o * gate
        return carry
    lax.fori_loop(0, seq // BLK // NB, body, 0)


def _b_attn_prompt(q, kv, sinks):
    b, seq, _ = q.shape
    pairs = B_HEADS * HEAD_DIM // LANES
    return pl.pallas_call(
        _b_attn_kernel,
        grid=(b, pairs),
        in_specs=[
            pl.BlockSpec(memory_space=pltpu.SMEM),
            pl.BlockSpec((1, seq, LANES), lambda bi, p: (bi, 0, p)),
            pl.BlockSpec((1, seq, LANES), lambda bi, p: (bi, 0, 0)),
            pl.BlockSpec((1, seq, LANES), lambda bi, p: (bi, 0, 1)),
        ],
        out_specs=pl.BlockSpec((1, seq, LANES), lambda bi, p: (bi, 0, p)),
        out_shape=jax.ShapeDtypeStruct((b, seq, B_HEADS * HEAD_DIM), F32),
        scratch_shapes=[pltpu.VMEM((seq, LANES), BF16), pltpu.VMEM((seq, LANES), BF16),
                        pltpu.VMEM((3, BLK, 2 * BLK), F32)],
        compiler_params=_cparams(("arbitrary", "arbitrary")),
        name="b_attn_prompt",
    )(sinks.reshape(1, B_HEADS), q, kv, kv)


FF_CHUNK = 256


def _silu_gate(cg, cv):
    return (cg / (1.0 + jnp.exp(-cg))) * cv


def _ffn_prompt_kernel(h_ref, o_ref, wo_ref, g_ref, wup_ref, cw_ref, cb_ref, wdn_ref,
                       hout_ref, tail_ref):
    i = pl.program_id(1)
    tm = h_ref.shape[1]

    @pl.when(i == 0)
    def _():
        tail_ref[...] = jnp.zeros_like(tail_ref)

    h2 = h_ref[0] + jnp.dot(o_ref[0].astype(BF16), wo_ref[...], preferred_element_type=F32)
    ms = jnp.mean(h2 * h2, axis=-1, keepdims=True)
    hn = (h2 * lax.rsqrt(ms + EPS) * g_ref[...]).astype(BF16)
    row = lax.broadcasted_iota(jnp.int32, (tm, 1), 0)
    acc = jnp.zeros((tm, D_MODEL), F32)
    for j in range(D_FF // FF_CHUNK):
        parts = []
        for base in (0, D_FF):
            c0 = base + j * FF_CHUNK
            sl = slice(c0, c0 + FF_CHUNK)
            u = jnp.dot(hn, wup_ref[:, sl], preferred_element_type=F32)
            prev = tail_ref[0, :, sl]
            p1 = prev[7:8, :]
            p2 = prev[6:7, :]
            u1 = jnp.where(row == 0, p1, pltpu.roll(u, 1, 0))
            u2 = jnp.where(row == 0, p2, jnp.where(row == 1, p1, pltpu.roll(u, 2, 0)))
            cwj = cw_ref[:, sl]
            parts.append(cwj[0:1, :] * u2 + cwj[1:2, :] * u1 + cwj[2:3, :] * u + cb_ref[:, sl])
            tail_ref[0, :, sl] = u[tm - 8:tm, :]
        act = _silu_gate(parts[0], parts[1]).astype(BF16)
        acc = acc + jnp.dot(act, wdn_ref[j * FF_CHUNK:(j + 1) * FF_CHUNK, :],
                            preferred_element_type=F32)
    hout_ref[0] = h2 + acc


def _ffn_prompt(h, o, w_out, g, w_up, conv_w, conv_b, w_down, tm=512):
    b, seq, _ = h.shape
    ko = o.shape[-1]
    const = lambda bi, i: (0, 0)
    once = pl.Buffered(1)
    return pl.pallas_call(
        _ffn_prompt_kernel,
        grid=(b, seq // tm),
        in_specs=[
            pl.BlockSpec((1, tm, D_MODEL), lambda bi, i: (bi, i, 0)),
            pl.BlockSpec((1, tm, ko), lambda bi, i: (bi, i, 0)),
            pl.BlockSpec((ko, D_MODEL), const, pipeline_mode=once),
            pl.BlockSpec((1, D_MODEL), const),
            pl.BlockSpec((D_MODEL, 2 * D_FF), const, pipeline_mode=once),
            pl.BlockSpec((3, 2 * D_FF), const),
            pl.BlockSpec((1, 2 * D_FF), const),
            pl.BlockSpec((D_FF, D_MODEL), const, pipeline_mode=once),
        ],
        out_specs=[
            pl.BlockSpec((1, tm, D_MODEL), lambda bi, i: (bi, i, 0)),
            pl.BlockSpec((1, 8, 2 * D_FF), lambda bi, i: (bi, 0, 0)),
        ],
        out_shape=[jax.ShapeDtypeStruct((b, seq, D_MODEL), F32),
                   jax.ShapeDtypeStruct((b, 8, 2 * D_FF), F32)],
        compiler_params=_cparams(("arbitrary", "arbitrary")),
        name="ffn_prompt",
    )(h, o, w_out, g.reshape(1, D_MODEL), w_up, conv_w, conv_b.reshape(1, 2 * D_FF), w_down)


def _ffn_sample_kernel(h_ref, ot_ref, lt_ref, wo_ref, g_ref, wup_ref, cw_ref, cb_ref, wdn_ref,
                       st0_ref, st1_ref, hout_ref, u_ref, *, n_groups):
    if n_groups == 1:
        ot = ot_ref[0]
    else:
        m = lt_ref[0]
        for gi in range(1, n_groups):
            m = jnp.maximum(m, lt_ref[gi])
        es = [jnp.exp(lt_ref[gi] - m) for gi in range(n_groups)]
        tot = es[0]
        for gi in range(1, n_groups):
            tot = tot + es[gi]
        ot = (es[0] / tot) * ot_ref[0]
        for gi in range(1, n_groups):
            ot = ot + (es[gi] / tot) * ot_ref[gi]
    o = ot.T.astype(BF16)
    h2 = h_ref[...] + jnp.dot(o, wo_ref[...], preferred_element_type=F32)
    ms = jnp.mean(h2 * h2, axis=-1, keepdims=True)
    hn = (h2 * lax.rsqrt(ms + EPS) * g_ref[...]).astype(BF16)
    acc = jnp.zeros(h2.shape, F32)
    for j in range(D_FF // FF_CHUNK):
        parts = []
        for base in (0, D_FF):
            c0 = base + j * FF_CHUNK
            sl = slice(c0, c0 + FF_CHUNK)
            u = jnp.dot(hn, wup_ref[:, sl], preferred_element_type=F32)
            u_ref[:, sl] = u
            cwj = cw_ref[:, sl]
            parts.append(cwj[0:1, :] * st0_ref[:, sl] + cwj[1:2, :] * st1_ref[:, sl]
                         + cwj[2:3, :] * u + cb_ref[:, sl])
        act = _silu_gate(parts[0], parts[1]).astype(BF16)
        acc = acc + jnp.dot(act, wdn_ref[j * FF_CHUNK:(j + 1) * FF_CHUNK, :],
                            preferred_element_type=F32)
    hout_ref[...] = h2 + acc


def _ffn_sample(h, ot, lt, w_out, g, w_up, conv_w, conv_b, w_down, st0, st1):
    n = h.shape[0]
    n_groups = ot.shape[0]
    full = lambda a: pl.BlockSpec(a.shape, lambda i, nd=a.ndim: (0,) * nd)
    args = (h, ot, lt, w_out, g.reshape(1, D_MODEL), w_up, conv_w, conv_b.reshape(1, 2 * D_FF),
            w_down, st0, st1)
    return pl.pallas_call(
        functools.partial(_ffn_sample_kernel, n_groups=n_groups),
        grid=(1,),
        in_specs=[full(a) for a in args],
        out_specs=[pl.BlockSpec((n, D_MODEL), lambda i: (0, 0)),
                   pl.BlockSpec((n, 2 * D_FF), lambda i: (0, 0))],
        out_shape=[jax.ShapeDtypeStruct((n, D_MODEL), F32),
                   jax.ShapeDtypeStruct((n, 2 * D_FF), F32)],
        compiler_params=_cparams(("arbitrary",)),
        name="ffn_sample",
    )(*args)


def _sample_attn_kernel(*refs, nb, hb, rep, dil, wmin, has_new, shift, use_sink, aliased):
    refs = list(refs)
    sink_ref = refs.pop(0) if use_sink else None
    qt_ref, kt_ref, vt_ref, cache_ref = refs[:4]
    refs = refs[4:]
    if aliased:
        refs = refs[1:]
    ot_ref, lt_ref = refs[:2]
    newc_ref = refs[2] if shift else None

    nblk = pl.program_id(0)
    hc = pl.program_id(1)
    w = cache_ref.shape[-1]
    lane = lax.broadcasted_iota(jnp.int32, (1, LANES), 1)
    lanew = lax.broadcasted_iota(jnp.int32, (1, w), 1)
    valid = ((lanew & (dil - 1)) == 0) & (lanew >= wmin)
    last = lanew == (w - 1)
    q_rows = hb * rep * HEAD_DIM
    kv_rows = hb * HEAD_DIM
    q_base = pl.multiple_of(hc * q_rows, q_rows)
    kv_base = pl.multiple_of(hc * kv_rows, kv_rows)

    @pl.when((nblk == 0) & (hc == 0))
    def _():
        ot_ref[...] = jnp.zeros_like(ot_ref)
        lt_ref[...] = jnp.zeros_like(lt_ref)

    def per_sample(ni, carry):
        sel = lane == nblk * nb + ni

        def columns(ref, base, rows):
            return jnp.sum(jnp.where(sel, ref[pl.ds(base, rows), :], 0.0), axis=1, keepdims=True)

        qcols = columns(qt_ref, q_base, q_rows)
        if has_new:
            kcols = columns(kt_ref, kv_base, kv_rows)
            vcols = columns(vt_ref, kv_base, kv_rows)
        if rep == 1 and has_new and shift and not use_sink:
            kt3 = cache_ref[0, ni, 0]
            vt3 = cache_ref[0, ni, 1]
            q3 = qcols.reshape(hb, HEAD_DIM, 1)
            k3n = kcols.reshape(hb, HEAD_DIM, 1)
            v3n = vcols.reshape(hb, HEAD_DIM, 1)
            s = jnp.where(valid, jnp.sum(kt3 * q3, axis=1), NEG)
            s_new = jnp.sum(q3 * k3n, axis=1)
            m = jnp.maximum(jnp.max(s, axis=1, keepdims=True), s_new)
            e = jnp.exp(s - m)
            e_new = jnp.exp(s_new - m)
            den = jnp.sum(e, axis=1, keepdims=True) + e_new
            o3 = jnp.sum(vt3 * e[:, None, :], axis=2, keepdims=True) + e_new[:, :, None] * v3n
            o3 = o3 / den[:, :, None]
            lse3 = jnp.broadcast_to((m + jnp.log(den))[:, :, None], (hb, HEAD_DIM, 1))
            rows = pl.ds(q_base, q_rows)
            ot_ref[rows, :] = jnp.where(sel, o3.reshape(q_rows, 1), ot_ref[rows, :])
            lt_ref[rows, :] = jnp.where(sel, lse3.reshape(q_rows, 1), lt_ref[rows, :])
            newc_ref[0, ni, 0] = jnp.where(last, k3n, pltpu.roll(kt3, w - 1, 2))
            newc_ref[0, ni, 1] = jnp.where(last, v3n, pltpu.roll(vt3, w - 1, 2))
            return carry
        for j in range(hb):
            kt = cache_ref[0, ni, 0, j]
            vt = cache_ref[0, ni, 1, j]
            if has_new:
                kcol = kcols[j * HEAD_DIM:(j + 1) * HEAD_DIM]
                vcol = vcols[j * HEAD_DIM:(j + 1) * HEAD_DIM]
            for t in range(rep):
                hl = j * rep + t
                qcol = qcols[hl * HEAD_DIM:(hl + 1) * HEAD_DIM]
                s = jnp.sum(kt * qcol, axis=0, keepdims=True)
                s = jnp.where(valid, s, NEG)
                m = jnp.max(s, axis=1, keepdims=True)
                if has_new:
                    s_new = jnp.sum(qcol * kcol, axis=0, keepdims=True)
                    m = jnp.maximum(m, s_new)
                e = jnp.exp(s - m)
                den = jnp.sum(e, axis=1, keepdims=True)
                o = jnp.sum(vt * e, axis=1, keepdims=True)
                if has_new:
                    e_new = jnp.exp(s_new - m)
                    den = den + e_new
                    o = o + e_new * vcol
                o = o / den
                lse = m + jnp.log(den)
                if use_sink:
                    o = o * (1.0 / (1.0 + jnp.exp(sink_ref[0, hc * hb * rep + hl] - lse)))
                rows = pl.ds(pl.multiple_of(q_base + hl * HEAD_DIM, HEAD_DIM), HEAD_DIM)
                ot_ref[rows, :] = jnp.where(sel, o, ot_ref[rows, :])
                lt_ref[rows, :] = jnp.where(sel, lse, lt_ref[rows, :])
            if shift:
                newc_ref[0, ni, 0, j] = jnp.where(last, kcol, pltpu.roll(kt, w - 1, 1))
                newc_ref[0, ni, 1, j] = jnp.where(last, vcol, pltpu.roll(vt, w - 1, 1))
        return carry
    lax.fori_loop(0, nb, per_sample, 0, unroll=2 if nb % 2 == 0 else 1)


def _sample_attn(qt, kt, vt, cache, layer, *, rep, dil, wmin, has_new, shift, sinks=None,
                 prev_out=None):
    nl, n, _, hkv, _, w = cache.shape
    hq = hkv * rep
    hb = hkv if w * hkv <= 8192 else 4
    per_sample_bytes = 2 * hb * HEAD_DIM * w * 4
    nb = max(1, min(8, SAMPLE_BLOCK_BYTES // per_sample_bytes))
    use_sink = sinks is not None
    aliased = prev_out is not None
    const2 = lambda ni, hc: (0, 0)
    cache_spec = pl.BlockSpec((1, nb, 2, hb, HEAD_DIM, w), lambda ni, hc: (layer, ni, 0, hc, 0, 0))
    in_specs, args = [], []
    if use_sink:
        in_specs.append(pl.BlockSpec(memory_space=pltpu.SMEM))
        args.append(sinks.reshape(1, hq))
    in_specs += [pl.BlockSpec((hq * HEAD_DIM, n), const2), pl.BlockSpec((hkv * HEAD_DIM, n), const2),
                 pl.BlockSpec((hkv * HEAD_DIM, n), const2), cache_spec]
    args += [qt, kt, vt, cache]
    io_alias = {}
    if aliased:
        in_specs.append(pl.BlockSpec(memory_space=pl.ANY))
        io_alias = {len(args): 2}
        args.append(prev_out)
    out_specs = [pl.BlockSpec((hq * HEAD_DIM, n), const2), pl.BlockSpec((hq * HEAD_DIM, n), const2)]
    out_shape = [jax.ShapeDtypeStruct((hq * HEAD_DIM, n), F32)] * 2
    if shift:
        out_specs.append(cache_spec)
        out_shape.append(jax.ShapeDtypeStruct(cache.shape, F32))
    return pl.pallas_call(
        functools.partial(_sample_attn_kernel, nb=nb, hb=hb, rep=rep, dil=dil, wmin=wmin,
                          has_new=has_new, shift=shift, use_sink=use_sink, aliased=aliased),
        grid=(n // nb, hkv // hb),
        in_specs=in_specs,
        out_specs=out_specs,
        out_shape=out_shape,
        input_output_aliases=io_alias,
        compiler_params=_cparams(("arbitrary", "arbitrary")),
        name="sample_attn",
    )(*args)


def _rope_tables(pos):
    half = HEAD_DIM // 2
    inv_freq = ROPE_THETA ** (-jnp.arange(half, dtype=F32) / half)
    ang = pos.astype(F32)[:, None] * inv_freq[None, :]
    cos, sin = jnp.cos(ang), jnp.sin(ang)
    idx = jnp.arange(LANES) % half
    first = (jnp.arange(LANES) % HEAD_DIM) < half
    c = cos[:, idx]
    s = sin[:, idx]
    return c, jnp.where(first[None, :], -s, 0.0), jnp.where(first[None, :], 0.0, s)


def _tile_gain(g, heads):
    return jnp.tile(g.astype(F32), heads)


def kernel(x_prompt, x_sample, cache_a_kv_w128, cache_a_kv_w512, cache_a_kv_w2048, cache_b_kv,
           state_ffn_conv, attn_norm_g, ffn_norm_g, a_w_in, a_q_norm_g, a_k_norm_g, a_w_out,
           b_kv_norm_g, b_w_kv, b_k_norm_g, b_w_q, b_q_norm_g, b_sinks, b_w_out,
           ffn_w_up, ffn_conv_w, ffn_conv_b, ffn_w_down):
    batch, seq, _ = x_prompt.shape
    n_s = x_sample.shape[0]
    depth = attn_norm_g.shape[0]
    n_a = a_w_in.shape[0]
    scale = HEAD_DIM ** -0.5

    a_w_in_b = a_w_in.astype(BF16)
    a_w_out_b = a_w_out.astype(BF16)
    b_w_kv_b = b_w_kv.astype(BF16)
    b_w_q_b = b_w_q.astype(BF16)
    b_w_out_b = b_w_out.astype(BF16)
    w_up_b = ffn_w_up.astype(BF16)
    w_down_b = ffn_w_down.astype(BF16)

    tabs_p = _rope_tables(jnp.arange(seq))
    tabs_s = _rope_tables(jnp.full((n_s,), PAST_LEN))

    def a_gains(layer):
        gq = jnp.concatenate([_tile_gain(a_q_norm_g[layer, g], A_HEADS) for g in range(N_GROUPS)]) * scale
        gk = jnp.concatenate([_tile_gain(a_k_norm_g[layer, g], A_HEADS) for g in range(N_GROUPS)])
        return jnp.concatenate([gq, gk]).reshape(1, -1)

    kv_gain = _tile_gain(b_k_norm_g, B_KV_HEADS).reshape(1, -1)

    caches_t = [jnp.transpose(c, (0, 1, 2, 4, 5, 3))
                for c in (cache_a_kv_w128, cache_a_kv_w512, cache_a_kv_w2048)]
    cache_b_t = jnp.transpose(cache_b_kv, (0, 1, 3, 4, 2))[None]

    hp = x_prompt
    hs = x_sample.reshape(n_s, D_MODEL)
    a_rows_p = [[] for _ in range(N_GROUPS)]
    new_caches = [None] * N_GROUPS
    conv_p, conv_s = [], []
    kv_p = None
    new_b = None
    kv_st = None
    n_q = N_GROUPS * A_WIDTH

    for layer in range(depth):
        if layer < n_a:
            gains = a_gains(layer)
            qkv = _proj(hp.reshape(batch * seq, D_MODEL), attn_norm_g[layer], a_w_in_b[layer], tabs_p,
                        gains, 2 * n_q // LANES, 256, seq // 256).reshape(batch, seq, 3 * n_q)
            for g, win in enumerate(A_WINDOWS):
                keep = min(win, seq)
                kk = qkv[:, seq - keep:, n_q + g * A_WIDTH:n_q + (g + 1) * A_WIDTH]
                vv = qkv[:, seq - keep:, 2 * n_q + g * A_WIDTH:2 * n_q + (g + 1) * A_WIDTH]
                a_rows_p[g].append(jnp.stack([kk, vv], axis=1).reshape(batch, 2, keep, A_HEADS, HEAD_DIM))
            o_p = _a_attn_prompt(qkv)
            w_out_l = a_w_out_b[layer]
            qkv_s = _proj(hs, attn_norm_g[layer], a_w_in_b[layer], tabs_s, gains, 2 * n_q // LANES,
                          n_s, 1)
            qkv_st = qkv_s.T
            ots, lts = [], []
            for g in range(N_GROUPS):
                qt = qkv_st[g * A_WIDTH:(g + 1) * A_WIDTH]
                kt = qkv_st[n_q + g * A_WIDTH:n_q + (g + 1) * A_WIDTH]
                vt = qkv_st[2 * n_q + g * A_WIDTH:2 * n_q + (g + 1) * A_WIDTH]
                ot, lt, new_caches[g] = _sample_attn(
                    qt, kt, vt, caches_t[g], layer, rep=1, dil=A_DILS[g], wmin=0, has_new=True,
                    shift=True, prev_out=new_caches[g])
                ots.append(ot)
                lts.append(lt)
            ot_s = jnp.stack(ots)
            lt_s = jnp.stack(lts)
        else:
            j = layer - n_a
            q_gain = (_tile_gain(b_q_norm_g[j], B_HEADS) * scale).reshape(1, -1)
            rep_b = B_HEADS // B_KV_HEADS
            q_s = _proj(hs, attn_norm_g[layer], b_w_q_b[j], tabs_s, q_gain, B_HEADS * HEAD_DIM // LANES,
                        n_s, 1)
            if layer == n_a:
                kv_p = _proj(hp.reshape(batch * seq, D_MODEL), b_kv_norm_g, b_w_kv_b, tabs_p, kv_gain,
                             1, 256, seq // 256).reshape(batch, seq, 2 * LANES)
                kv_st = _proj(hs, b_kv_norm_g, b_w_kv_b, tabs_s, kv_gain, 1, n_s, 1).T
                ot, lt, new_b = _sample_attn(q_s.T, kv_st[:LANES], kv_st[LANES:], cache_b_t, 0, rep=rep_b,
                                             dil=1, wmin=1, has_new=True, shift=True, sinks=b_sinks[j])
            else:
                ot, lt = _sample_attn(q_s.T, kv_st[:LANES], kv_st[LANES:], new_b, 0, rep=rep_b, dil=1,
                                      wmin=0, has_new=False, shift=False, sinks=b_sinks[j])
            q_p = _proj(hp.reshape(batch * seq, D_MODEL), attn_norm_g[layer], b_w_q_b[j], tabs_p, q_gain,
                        B_HEADS * HEAD_DIM // LANES, 256, seq // 256).reshape(batch, seq, -1)
            o_p = _b_attn_prompt(q_p, kv_p, b_sinks[j])
            w_out_l = b_w_out_b[j]
            ot_s = ot[None]
            lt_s = lt[None]

        hp, tail = _ffn_prompt(hp, o_p, w_out_l, ffn_norm_g[layer], w_up_b[layer], ffn_conv_w[layer],
                               ffn_conv_b[layer], w_down_b[layer])
        conv_p.append(tail[:, 6:8, :])
        st0 = state_ffn_conv[layer, :, 0, :]
        st1 = state_ffn_conv[layer, :, 1, :]
        hs, u_s = _ffn_sample(hs, ot_s, lt_s, w_out_l, ffn_norm_g[layer], w_up_b[layer],
                              ffn_conv_w[layer], ffn_conv_b[layer], w_down_b[layer], st0, st1)
        conv_s.append(jnp.stack([st1, u_s], axis=1))

    a_out_p = [jnp.stack(r, axis=0) for r in a_rows_p]
    a_out_s = [jnp.transpose(c, (0, 1, 2, 5, 3, 4)) for c in new_caches]
    b_p = jnp.stack([kv_p[:, seq - B_WINDOW:, :LANES], kv_p[:, seq - B_WINDOW:, LANES:]], axis=1)
    b_p = b_p.reshape(batch, 2, B_WINDOW, B_KV_HEADS, HEAD_DIM)
    b_s = jnp.transpose(new_b[0], (0, 1, 4, 2, 3))
    return (hp, hs.reshape(n_s, 1, D_MODEL), a_out_p[0], a_out_s[0], a_out_p[1], a_out_s[1],
            a_out_p[2], a_out_s[2], b_p, b_s, jnp.stack(conv_p, axis=0), jnp.stack(conv_s, axis=0))
```

```python
import functools

import jax
import jax.numpy as jnp
from jax import lax
from jax.experimental import pallas as pl
from jax.experimental.pallas import tpu as pltpu

D_MODEL = 1024
HEAD_DIM = 64
N_GROUPS = 3
A_DILS = (1, 4, 16)
A_WINDOWS = (128, 512, 2048)
A_HEADS = 8
A_WIDTH = A_HEADS * HEAD_DIM
B_HEADS = 16
B_KV_HEADS = 2
B_WINDOW = 128
D_FF = 2816
ROPE_THETA = 10000.0
EPS = 1e-6
PAST_LEN = 8192
LANES = 128
BLK = 128
NEG = -1e30
VMEM_LIMIT = 56 * 1024 * 1024
SAMPLE_BLOCK_BYTES = 4 * 1024 * 1024

F32 = jnp.float32
BF16 = jnp.bfloat16
_NT = (((1,), (1,)), ((), ()))


def _cparams(sem):
    return pltpu.CompilerParams(dimension_semantics=sem, vmem_limit_bytes=VMEM_LIMIT)


def _proj_kernel(h_ref, g_ref, w_ref, c_ref, s1_ref, s2_ref, gain_ref, bd_ref, *rest,
                 n_rope, chunk, n_kvt, n_alias, seq_tiles):
    o_ref = rest[n_alias]
    kvt_refs = rest[n_alias + 1:]
    assert len(kvt_refs) == n_kvt
    x = h_ref[...]
    ms = jnp.mean(x * x, axis=-1, keepdims=True)
    hn = (x * lax.rsqrt(ms + EPS) * g_ref[...]).astype(BF16)
    cos = c_ref[...]
    sin_lo = s1_ref[...]
    sin_hi = s2_ref[...]
    bd = bd_ref[...]
    d_out = o_ref.shape[-1]
    per = chunk // LANES
    n_chunks = d_out // chunk

    def main(c0):
        return jnp.dot(hn, w_ref[:, c0 * chunk:(c0 + 1) * chunk], preferred_element_type=F32)

    acc_next = main(0)
    for c0 in range(n_chunks):
        acc = acc_next
        acc_next = main(c0 + 1) if c0 + 1 < n_chunks else None
        for j in range(per):
            blk = c0 * per + j
            xc = acc[:, j * LANES:(j + 1) * LANES]
            if blk < n_rope:
                msq = jnp.dot((xc * xc).astype(BF16), bd, preferred_element_type=F32)
                y = xc * lax.rsqrt(msq + EPS) * gain_ref[:, blk * LANES:(blk + 1) * LANES]
                xc = (y * cos + pltpu.roll(y, 96, 1) * sin_lo + pltpu.roll(y, 32, 1) * sin_hi)
            o_ref[:, blk * LANES:(blk + 1) * LANES] = xc

    tm = h_ref.shape[0]
    n_q = d_out // 3
    it = pl.program_id(0) % seq_tiles
    for g, ref in enumerate(kvt_refs):
        bw = ref.shape[-1]
        first = seq_tiles - max(A_WINDOWS[g] // tm, 1)

        def emit(g=g, ref=ref, bw=bw):
            for kv in range(2):
                c0 = (1 + kv) * n_q + g * A_WIDTH
                ref[0, 0, kv] = o_ref[:, c0:c0 + A_WIDTH].T[:, tm - bw:]
        if first == 0:
            emit()
        else:
            pl.when(it >= first)(emit)


def _proj(h2d, norm_g, w, tabs, gains, n_rope, tm, seq_tiles, kvt_layer=None, kvt_layers=0,
          kvt_prev=None):
    n, _ = h2d.shape
    d_out = w.shape[1]
    chunk = 512 if d_out % 512 == 0 else 256
    cos, sin_lo, sin_hi = tabs
    gw = max(n_rope, 1) * LANES
    if gains.shape[1] < gw:
        gains = jnp.ones((1, gw), F32)
    bd = ((jnp.arange(LANES)[:, None] // HEAD_DIM == jnp.arange(LANES)[None, :] // HEAD_DIM)
          .astype(F32) / HEAD_DIM).astype(BF16)
    tab_spec = pl.BlockSpec((tm, LANES), lambda i: (i % seq_tiles, 0))
    in_specs = [
        pl.BlockSpec((tm, D_MODEL), lambda i: (i, 0)),
        pl.BlockSpec((1, D_MODEL), lambda i: (0, 0)),
        pl.BlockSpec((D_MODEL, d_out), lambda i: (0, 0)),
        tab_spec, tab_spec, tab_spec,
        pl.BlockSpec((1, gw), lambda i: (0, 0)),
        pl.BlockSpec((LANES, LANES), lambda i: (0, 0)),
    ]
    args = [h2d, norm_g.reshape(1, D_MODEL), w, cos, sin_lo, sin_hi, gains, bd]
    out_specs = [pl.BlockSpec((tm, d_out), lambda i: (i, 0))]
    out_shape = [jax.ShapeDtypeStruct((n, d_out), F32)]
    io_alias = {}
    n_kvt = 0
    if kvt_layer is not None:
        n_kvt = N_GROUPS
        batch = n // (tm * seq_tiles)
        for g, win in enumerate(A_WINDOWS):
            bw = min(tm, win)
            first = seq_tiles - max(win // tm, 1)
            out_specs.append(pl.BlockSpec(
                (1, 1, 2, A_WIDTH, bw),
                lambda i, first=first: (kvt_layer, i // seq_tiles, 0, 0,
                                        jnp.maximum(i % seq_tiles - first, 0))))
            out_shape.append(jax.ShapeDtypeStruct((kvt_layers, batch, 2, A_WIDTH, win), F32))
        if kvt_prev is not None:
            for g in range(N_GROUPS):
                in_specs.append(pl.BlockSpec(memory_space=pl.ANY))
                io_alias[len(args)] = 1 + g
                args.append(kvt_prev[g])
    outs = pl.pallas_call(
        functools.partial(_proj_kernel, n_rope=n_rope, chunk=chunk, n_kvt=n_kvt,
                          n_alias=len(io_alias), seq_tiles=seq_tiles),
        grid=(n // tm,),
        in_specs=in_specs,
        out_specs=out_specs,
        out_shape=out_shape,
        input_output_aliases=io_alias,
        compiler_params=_cparams(("arbitrary",)),
        name="proj",
    )(*args)
    return (outs[0], list(outs[1:])) if n_kvt else outs[0]


NB = 8


def _band_bias(off, nk, max_dist):
    qi = lax.broadcasted_iota(jnp.int32, (BLK, 1), 0)
    kj = lax.broadcasted_iota(jnp.int32, (1, nk), 1)
    dist = off + qi - kj
    return jnp.where((dist >= 0) & (dist <= max_dist), 0.0, NEG).astype(F32)


def _init_bias(bias_scr, max_dist):
    bias_scr[0] = _band_bias(0, 2 * BLK, max_dist)
    bias_scr[1] = _band_bias(BLK, 2 * BLK, max_dist)
    qi = lax.broadcasted_iota(jnp.int32, (BLK, 1), 0)
    kj = lax.broadcasted_iota(jnp.int32, (1, 2 * BLK), 1)
    bias_scr[2] = jnp.where((kj >= BLK) & (kj - BLK <= qi), 0.0, NEG).astype(F32)


def _attend(blocks):
    lane = lax.broadcasted_iota(jnp.int32, (1, LANES), 1)
    lo = lane < HEAD_DIM
    hi = jnp.logical_not(lo)
    scores = []
    for q, k, _, _ in blocks:
        for sel in (lo, hi):
            qa = jnp.where(sel, q, 0.0).astype(BF16)
            scores.append(lax.dot_general(qa, k, _NT, preferred_element_type=F32))
    parts = []
    for bi, (_, _, v, bias) in enumerate(blocks):
        for a in range(2):
            s = scores[2 * bi + a] + bias
            m = jnp.max(s, axis=1, keepdims=True)
            e = jnp.exp(s - m)
            den = jnp.sum(e, axis=1, keepdims=True)
            pv = jnp.dot(e.astype(BF16), v, preferred_element_type=F32)
            parts.append((pv, den, m))
    outs = []
    for bi in range(len(blocks)):
        (pv0, d0, m0), (pv1, d1, m1) = parts[2 * bi], parts[2 * bi + 1]
        den = jnp.where(lo, d0, d1)
        o = jnp.where(lo, pv0, pv1) / den
        lse = jnp.where(lo, m0, m1) + jnp.log(den)
        outs.append((o, lse))
    return outs


def _a_attn_kernel(q0, q1, q2, k0, k1, k2, v0, v1, v2, o_ref, o_scr, l_scr, bias_scr):
    seq = q0.shape[1]

    @pl.when((pl.program_id(0) == 0) & (pl.program_id(1) == 0))
    def _():
        _init_bias(bias_scr, BLK)

    def g0_body(it, carry):
        blocks, starts = [], []
        for j in range(NB):
            i = it * NB + j
            qs = pl.multiple_of(i * BLK, BLK)
            ks_ = pl.multiple_of(jnp.maximum(i - 1, 0) * BLK, BLK)
            if j == 0:
                bias = jnp.where(it == 0, bias_scr[0], bias_scr[1])
            else:
                bias = bias_scr[1]
            blocks.append((q0[0, pl.ds(qs, BLK), :], k0[0, pl.ds(ks_, 2 * BLK), :].astype(BF16),
                           v0[0, pl.ds(ks_, 2 * BLK), :].astype(BF16), bias))
            starts.append(qs)
        for qs, (o, l) in zip(starts, _attend(blocks)):
            o_scr[0, pl.ds(qs, BLK), :] = o
            l_scr[0, pl.ds(qs, BLK), :] = l
        return carry
    lax.fori_loop(0, seq // BLK // NB, g0_body, 0)

    d1 = A_DILS[1]
    nb1 = seq // d1 // BLK

    spi = NB // nb1

    def g1_body(it, carry):
        blocks, starts = [], []
        for j in range(spi):
            r = it * spi + j
            for i in range(nb1):
                kb = max(i - 1, 0)
                blocks.append((q1[0, pl.ds(i * BLK * d1 + r, BLK, stride=d1), :],
                               k1[0, pl.ds(kb * BLK * d1 + r, 2 * BLK, stride=d1), :].astype(BF16),
                               v1[0, pl.ds(kb * BLK * d1 + r, 2 * BLK, stride=d1), :].astype(BF16),
                               bias_scr[min(i, 1)]))
                starts.append(i * BLK * d1 + r)
        for qs, (o, l) in zip(starts, _attend(blocks)):
            o_scr[1, pl.ds(qs, BLK, stride=d1), :] = o
            l_scr[1, pl.ds(qs, BLK, stride=d1), :] = l
        return carry
    lax.fori_loop(0, d1 // spi, g1_body, 0)

    d2 = A_DILS[2]

    def g2_body(it, carry):
        blocks = []
        for j in range(0, NB, 2):
            ra = it * NB + j
            rb = ra + 1
            kk = jnp.concatenate([k2[0, pl.ds(ra, BLK, stride=d2), :],
                                  k2[0, pl.ds(rb, BLK, stride=d2), :]], axis=0).astype(BF16)
            vv = jnp.concatenate([v2[0, pl.ds(ra, BLK, stride=d2), :],
                                  v2[0, pl.ds(rb, BLK, stride=d2), :]], axis=0).astype(BF16)
            blocks.append((q2[0, pl.ds(ra, BLK, stride=d2), :], kk, vv, bias_scr[0]))
            blocks.append((q2[0, pl.ds(rb, BLK, stride=d2), :], kk, vv, bias_scr[2]))
        for j, (o, l) in enumerate(_attend(blocks)):
            r = it * NB + j
            o_scr[2, pl.ds(r, BLK, stride=d2), :] = o
            l_scr[2, pl.ds(r, BLK, stride=d2), :] = l
        return carry
    lax.fori_loop(0, d2 // NB, g2_body, 0)

    rows = 256

    def merge_body(i, carry):
        rs = pl.multiple_of(i * rows, rows)
        la = l_scr[0, pl.ds(rs, rows), :]
        lb = l_scr[1, pl.ds(rs, rows), :]
        lc = l_scr[2, pl.ds(rs, rows), :]
        m = jnp.maximum(jnp.maximum(la, lb), lc)
        ea = jnp.exp(la - m)
        eb = jnp.exp(lb - m)
        ec = jnp.exp(lc - m)
        tot = ea + eb + ec
        o = (ea / tot) * o_scr[0, pl.ds(rs, rows), :]
        o = o + (eb / tot) * o_scr[1, pl.ds(rs, rows), :]
        o = o + (ec / tot) * o_scr[2, pl.ds(rs, rows), :]
        o_ref[0, pl.ds(rs, rows), :] = o
        return carry
    lax.fori_loop(0, seq // rows, merge_body, 0)


def _a_attn_prompt(qkv):
    b, seq, _ = qkv.shape
    pairs = A_WIDTH // LANES
    nq = N_GROUPS * pairs

    def spec(base, g):
        return pl.BlockSpec((1, seq, LANES), lambda bi, p, base=base, g=g: (bi, 0, base + g * pairs + p))
    in_specs = ([spec(0, g) for g in range(N_GROUPS)] + [spec(nq, g) for g in range(N_GROUPS)]
                + [spec(2 * nq, g) for g in range(N_GROUPS)])
    return pl.pallas_call(
        _a_attn_kernel,
        grid=(b, pairs),
        in_specs=in_specs,
        out_specs=pl.BlockSpec((1, seq, LANES), lambda bi, p: (bi, 0, p)),
        out_shape=jax.ShapeDtypeStruct((b, seq, A_WIDTH), F32),
        scratch_shapes=[pltpu.VMEM((N_GROUPS, seq, LANES), F32),
                        pltpu.VMEM((N_GROUPS, seq, LANES), F32),
                        pltpu.VMEM((3, BLK, 2 * BLK), F32)],
        compiler_params=_cparams(("arbitrary", "arbitrary")),
        name="a_attn_prompt",
    )(*([qkv] * 9))


def _b_attn_kernel(sink_ref, q_ref, k_ref, v_ref, o_ref, kdup_scr, vdup_scr, bias_scr):
    seq = q_ref.shape[1]
    p = pl.program_id(1)
    c = p // (B_HEADS // B_KV_HEADS // 2)
    lane = lax.broadcasted_iota(jnp.int32, (1, LANES), 1)
    lo = lane < HEAD_DIM

    @pl.when((pl.program_id(0) == 0) & (p == 0))
    def _():
        _init_bias(bias_scr, B_WINDOW - 1)

    keep = (lane >= HEAD_DIM).astype(jnp.int32) == c
    kdup_scr[...] = jnp.where(keep, k_ref[0], pltpu.roll(k_ref[0], HEAD_DIM, 1)).astype(BF16)
    vdup_scr[...] = jnp.where(keep, v_ref[0], pltpu.roll(v_ref[0], HEAD_DIM, 1)).astype(BF16)
    sink = jnp.where(lo, sink_ref[0, 2 * p], sink_ref[0, 2 * p + 1])

    def body(it, carry):
        blocks, starts = [], []
        for j in range(NB):
            i = it * NB + j
            qs = pl.multiple_of(i * BLK, BLK)
            ks_ = pl.multiple_of(jnp.maximum(i - 1, 0) * BLK, BLK)
            if j == 0:
                bias = jnp.where(it == 0, bias_scr[0], bias_scr[1])
            else:
                bias = bias_scr[1]
            blocks.append((q_ref[0, pl.ds(qs, BLK), :], kdup_scr[pl.ds(ks_, 2 * BLK), :],
                           vdup_scr[pl.ds(ks_, 2 * BLK), :], bias))
            starts.append(qs)
        for qs, (o, l) in zip(starts, _attend(blocks)):
            gate = 1.0 / (1.0 + jnp.exp(sink - l))
            o_ref[0, pl.ds(qs, BLK), :] = o * gate
        return carry
    lax.fori_loop(0, seq // BLK // NB, body, 0)


def _b_attn_prompt(q, kv, sinks):
    b, seq, _ = q.shape
    pairs = B_HEADS * HEAD_DIM // LANES
    return pl.pallas_call(
        _b_attn_kernel,
        grid=(b, pairs),
        in_specs=[
            pl.BlockSpec(memory_space=pltpu.SMEM),
            pl.BlockSpec((1, seq, LANES), lambda bi, p: (bi, 0, p)),
            pl.BlockSpec((1, seq, LANES), lambda bi, p: (bi, 0, 0)),
            pl.BlockSpec((1, seq, LANES), lambda bi, p: (bi, 0, 1)),
        ],
        out_specs=pl.BlockSpec((1, seq, LANES), lambda bi, p: (bi, 0, p)),
        out_shape=jax.ShapeDtypeStruct((b, seq, B_HEADS * HEAD_DIM), F32),
        scratch_shapes=[pltpu.VMEM((seq, LANES), BF16), pltpu.VMEM((seq, LANES), BF16),
                        pltpu.VMEM((3, BLK, 2 * BLK), F32)],
        compiler_params=_cparams(("arbitrary", "arbitrary")),
        name="b_attn_prompt",
    )(sinks.reshape(1, B_HEADS), q, kv, kv)


FF_CHUNK = 256
DOWN_GROUP = 3


def _silu_gate(cg, cv):
    return (cg / (1.0 + jnp.exp(-cg))) * cv


def _ffn_prompt_kernel(h_ref, o_ref, wo_ref, g_ref, wup_ref, cw_ref, cb_ref, wdn_ref,
                       hout_ref, tail_ref):
    i = pl.program_id(1)
    tm = h_ref.shape[1]

    @pl.when(i == 0)
    def _():
        tail_ref[...] = jnp.zeros_like(tail_ref)

    h2 = h_ref[0] + jnp.dot(o_ref[0].astype(BF16), wo_ref[...], preferred_element_type=F32)
    ms = jnp.mean(h2 * h2, axis=-1, keepdims=True)
    hn = (h2 * lax.rsqrt(ms + EPS) * g_ref[...]).astype(BF16)
    row = lax.broadcasted_iota(jnp.int32, (tm, 1), 0)
    n_chunks = D_FF // FF_CHUNK

    def up(j):
        return [jnp.dot(hn, wup_ref[:, base + j * FF_CHUNK:base + (j + 1) * FF_CHUNK],
                        preferred_element_type=F32) for base in (0, D_FF)]

    row8 = row[0:8]
    acc = jnp.zeros((tm, D_MODEL), F32)
    us = up(0)
    pending = []
    for j in range(n_chunks):
        us_next = up(j + 1) if j + 1 < n_chunks else None
        parts = []
        for u, base in zip(us, (0, D_FF)):
            sl = slice(base + j * FF_CHUNK, base + (j + 1) * FF_CHUNK)
            prev = tail_ref[0, :, sl]
            p1 = prev[7:8, :]
            p2 = prev[6:7, :]
            r1 = pltpu.roll(u, 1, 0)
            r2 = pltpu.roll(u, 2, 0)
            u1 = jnp.concatenate([jnp.where(row8 == 0, p1, r1[0:8]), r1[8:]], axis=0)
            u2 = jnp.concatenate(
                [jnp.where(row8 == 0, p2, jnp.where(row8 == 1, p1, r2[0:8])), r2[8:]], axis=0)
            cwj = cw_ref[:, sl]
            parts.append(cwj[0:1, :] * u2 + cwj[1:2, :] * u1 + cwj[2:3, :] * u + cb_ref[:, sl])
            tail_ref[0, :, sl] = u[tm - 8:tm, :]
        pending.append(_silu_gate(parts[0], parts[1]).astype(BF16))
        if len(pending) == DOWN_GROUP or j + 1 == n_chunks:
            first = j + 1 - len(pending)
            act = pending[0] if len(pending) == 1 else jnp.concatenate(pending, axis=1)
            acc = acc + jnp.dot(act, wdn_ref[first * FF_CHUNK:(j + 1) * FF_CHUNK, :],
                                preferred_element_type=F32)
            pending = []
        us = us_next
    hout_ref[0] = h2 + acc


def _ffn_prompt(h, o, w_out, g, w_up, conv_w, conv_b, w_down, tm=512):
    b, seq, _ = h.shape
    ko = o.shape[-1]
    const = lambda bi, i: (0, 0)
    once = pl.Buffered(1)
    return pl.pallas_call(
        _ffn_prompt_kernel,
        grid=(b, seq // tm),
        in_specs=[
            pl.BlockSpec((1, tm, D_MODEL), lambda bi, i: (bi, i, 0)),
            pl.BlockSpec((1, tm, ko), lambda bi, i: (bi, i, 0)),
            pl.BlockSpec((ko, D_MODEL), const, pipeline_mode=once),
            pl.BlockSpec((1, D_MODEL), const),
            pl.BlockSpec((D_MODEL, 2 * D_FF), const, pipeline_mode=once),
            pl.BlockSpec((3, 2 * D_FF), const),
            pl.BlockSpec((1, 2 * D_FF), const),
            pl.BlockSpec((D_FF, D_MODEL), const, pipeline_mode=once),
        ],
        out_specs=[
            pl.BlockSpec((1, tm, D_MODEL), lambda bi, i: (bi, i, 0)),
            pl.BlockSpec((1, 8, 2 * D_FF), lambda bi, i: (bi, 0, 0)),
        ],
        out_shape=[jax.ShapeDtypeStruct((b, seq, D_MODEL), F32),
                   jax.ShapeDtypeStruct((b, 8, 2 * D_FF), F32)],
        compiler_params=_cparams(("arbitrary", "arbitrary")),
        name="ffn_prompt",
    )(h, o, w_out, g.reshape(1, D_MODEL), w_up, conv_w, conv_b.reshape(1, 2 * D_FF), w_down)


def _ffn_sample_kernel(h_ref, ot_ref, lt_ref, wo_ref, g_ref, wup_ref, cw_ref, cb_ref, wdn_ref,
                       st0_ref, st1_ref, hout_ref, u_ref, *, n_groups):
    if n_groups == 1:
        ot = ot_ref[0]
    else:
        m = lt_ref[0]
        for gi in range(1, n_groups):
            m = jnp.maximum(m, lt_ref[gi])
        es = [jnp.exp(lt_ref[gi] - m) for gi in range(n_groups)]
        tot = es[0]
        for gi in range(1, n_groups):
            tot = tot + es[gi]
        ot = (es[0] / tot) * ot_ref[0]
        for gi in range(1, n_groups):
            ot = ot + (es[gi] / tot) * ot_ref[gi]
    o = ot.T.astype(BF16)
    h2 = h_ref[...] + jnp.dot(o, wo_ref[...], preferred_element_type=F32)
    ms = jnp.mean(h2 * h2, axis=-1, keepdims=True)
    hn = (h2 * lax.rsqrt(ms + EPS) * g_ref[...]).astype(BF16)
    acc = jnp.zeros(h2.shape, F32)
    for j in range(D_FF // FF_CHUNK):
        parts = []
        for base in (0, D_FF):
            c0 = base + j * FF_CHUNK
            sl = slice(c0, c0 + FF_CHUNK)
            u = jnp.dot(hn, wup_ref[:, sl], preferred_element_type=F32)
            u_ref[:, sl] = u
            cwj = cw_ref[:, sl]
            parts.append(cwj[0:1, :] * st0_ref[:, sl] + cwj[1:2, :] * st1_ref[:, sl]
                         + cwj[2:3, :] * u + cb_ref[:, sl])
        act = _silu_gate(parts[0], parts[1]).astype(BF16)
        acc = acc + jnp.dot(act, wdn_ref[j * FF_CHUNK:(j + 1) * FF_CHUNK, :],
                            preferred_element_type=F32)
    hout_ref[...] = h2 + acc


def _ffn_sample(h, ot, lt, w_out, g, w_up, conv_w, conv_b, w_down, st0, st1):
    n = h.shape[0]
    n_groups = ot.shape[0]
    full = lambda a: pl.BlockSpec(a.shape, lambda i, nd=a.ndim: (0,) * nd)
    args = (h, ot, lt, w_out, g.reshape(1, D_MODEL), w_up, conv_w, conv_b.reshape(1, 2 * D_FF),
            w_down, st0, st1)
    return pl.pallas_call(
        functools.partial(_ffn_sample_kernel, n_groups=n_groups),
        grid=(1,),
        in_specs=[full(a) for a in args],
        out_specs=[pl.BlockSpec((n, D_MODEL), lambda i: (0, 0)),
                   pl.BlockSpec((n, 2 * D_FF), lambda i: (0, 0))],
        out_shape=[jax.ShapeDtypeStruct((n, D_MODEL), F32),
                   jax.ShapeDtypeStruct((n, 2 * D_FF), F32)],
        compiler_params=_cparams(("arbitrary",)),
        name="ffn_sample",
    )(*args)


def _sample_attn_kernel(*refs, nb, hb, rep, dil, wmin, has_new, shift, use_sink, aliased):
    refs = list(refs)
    sink_ref = refs.pop(0) if use_sink else None
    qt_ref, kt_ref, vt_ref, cache_ref = refs[:4]
    refs = refs[4:]
    if aliased:
        refs = refs[1:]
    ot_ref, lt_ref = refs[:2]
    newc_ref = refs[2] if shift else None

    nblk = pl.program_id(0)
    hc = pl.program_id(1)
    w = cache_ref.shape[-1]
    lane = lax.broadcasted_iota(jnp.int32, (1, LANES), 1)
    lanew = lax.broadcasted_iota(jnp.int32, (1, w), 1)
    valid = ((lanew & (dil - 1)) == 0) & (lanew >= wmin)
    last = lanew == (w - 1)
    q_rows = hb * rep * HEAD_DIM
    kv_rows = hb * HEAD_DIM
    q_base = pl.multiple_of(hc * q_rows, q_rows)
    kv_base = pl.multiple_of(hc * kv_rows, kv_rows)

    @pl.when((nblk == 0) & (hc == 0))
    def _():
        ot_ref[...] = jnp.zeros_like(ot_ref)
        lt_ref[...] = jnp.zeros_like(lt_ref)

    def per_sample(ni, carry):
        sel = lane == nblk * nb + ni

        def columns(ref, base, rows):
            return jnp.sum(jnp.where(sel, ref[pl.ds(base, rows), :], 0.0), axis=1, keepdims=True)

        qcols = columns(qt_ref, q_base, q_rows)
        if has_new:
            kcols = columns(kt_ref, kv_base, kv_rows)
            vcols = columns(vt_ref, kv_base, kv_rows)
        def attend(kt3, vt3, q3, k3n, v3n, sink2):
            g = q3.shape[0]
            s = jnp.where(valid, jnp.sum(kt3 * q3, axis=1), NEG)
            m = jnp.max(s, axis=1, keepdims=True)
            if has_new:
                s_new = jnp.sum(q3 * k3n, axis=1)
                m = jnp.maximum(m, s_new)
            e = jnp.exp(s - m)
            den = jnp.sum(e, axis=1, keepdims=True)
            o3 = jnp.sum(vt3 * e[:, None, :], axis=2, keepdims=True)
            if has_new:
                e_new = jnp.exp(s_new - m)
                den = den + e_new
                o3 = o3 + e_new[:, :, None] * v3n
            o3 = o3 / den[:, :, None]
            lse = m + jnp.log(den)
            if use_sink:
                o3 = o3 * (1.0 / (1.0 + jnp.exp(sink2 - lse)))[:, :, None]
            lse3 = jnp.broadcast_to(lse[:, :, None], (g, HEAD_DIM, 1))
            return o3.reshape(g * HEAD_DIM, 1), lse3.reshape(g * HEAD_DIM, 1)

        def put(base, rows, ocol, lcol):
            sl = pl.ds(pl.multiple_of(base, HEAD_DIM), rows)
            ot_ref[sl, :] = jnp.where(sel, ocol, ot_ref[sl, :])
            lt_ref[sl, :] = jnp.where(sel, lcol, lt_ref[sl, :])

        if rep == 1:
            kt3 = cache_ref[0, ni, 0]
            vt3 = cache_ref[0, ni, 1]
            k3n = kcols.reshape(hb, HEAD_DIM, 1) if has_new else None
            v3n = vcols.reshape(hb, HEAD_DIM, 1) if has_new else None
            sink2 = sink_ref[pl.ds(q_base // HEAD_DIM, hb), :] if use_sink else None
            ocol, lcol = attend(kt3, vt3, qcols.reshape(hb, HEAD_DIM, 1), k3n, v3n, sink2)
            put(q_base, q_rows, ocol, lcol)
            if shift:
                newc_ref[0, ni, 0] = jnp.where(last, k3n, pltpu.roll(kt3, w - 1, 2))
                newc_ref[0, ni, 1] = jnp.where(last, v3n, pltpu.roll(vt3, w - 1, 2))
            return carry
        for j in range(hb):
            kt = cache_ref[0, ni, 0, j]
            vt = cache_ref[0, ni, 1, j]
            kcol = kcols[j * HEAD_DIM:(j + 1) * HEAD_DIM] if has_new else None
            vcol = vcols[j * HEAD_DIM:(j + 1) * HEAD_DIM] if has_new else None
            q3 = qcols[j * rep * HEAD_DIM:(j + 1) * rep * HEAD_DIM].reshape(rep, HEAD_DIM, 1)
            head0 = q_base // HEAD_DIM + j * rep
            sink2 = sink_ref[pl.ds(head0, rep), :] if use_sink else None
            ocol, lcol = attend(kt[None], vt[None], q3, kcol[None] if has_new else None,
                                vcol[None] if has_new else None, sink2)
            put(q_base + j * rep * HEAD_DIM, rep * HEAD_DIM, ocol, lcol)
            if shift:
                newc_ref[0, ni, 0, j] = jnp.where(last, kcol, pltpu.roll(kt, w - 1, 1))
                newc_ref[0, ni, 1, j] = jnp.where(last, vcol, pltpu.roll(vt, w - 1, 1))
        return carry
    lax.fori_loop(0, nb, per_sample, 0, unroll=2 if nb % 2 == 0 else 1)


def _sample_attn(qt, kt, vt, cache, layer, *, rep, dil, wmin, has_new, shift, sinks=None,
                 prev_out=None):
    nl, n, _, hkv, _, w = cache.shape
    hq = hkv * rep
    hb = hkv if w * hkv <= 8192 else 4
    per_sample_bytes = 2 * hb * HEAD_DIM * w * 4
    nb = max(1, min(8, SAMPLE_BLOCK_BYTES // per_sample_bytes))
    use_sink = sinks is not None
    aliased = prev_out is not None
    const2 = lambda ni, hc: (0, 0)
    cache_spec = pl.BlockSpec((1, nb, 2, hb, HEAD_DIM, w), lambda ni, hc: (layer, ni, 0, hc, 0, 0))
    in_specs, args = [], []
    if use_sink:
        in_specs.append(pl.BlockSpec((hq, 1), const2))
        args.append(sinks.reshape(hq, 1))
    in_specs += [pl.BlockSpec((hq * HEAD_DIM, n), const2), pl.BlockSpec((hkv * HEAD_DIM, n), const2),
                 pl.BlockSpec((hkv * HEAD_DIM, n), const2), cache_spec]
    args += [qt, kt, vt, cache]
    io_alias = {}
    if aliased:
        in_specs.append(pl.BlockSpec(memory_space=pl.ANY))
        io_alias = {len(args): 2}
        args.append(prev_out)
    out_specs = [pl.BlockSpec((hq * HEAD_DIM, n), const2), pl.BlockSpec((hq * HEAD_DIM, n), const2)]
    out_shape = [jax.ShapeDtypeStruct((hq * HEAD_DIM, n), F32)] * 2
    if shift:
        out_specs.append(cache_spec)
        out_shape.append(jax.ShapeDtypeStruct(cache.shape, F32))
    return pl.pallas_call(
        functools.partial(_sample_attn_kernel, nb=nb, hb=hb, rep=rep, dil=dil, wmin=wmin,
                          has_new=has_new, shift=shift, use_sink=use_sink, aliased=aliased),
        grid=(n // nb, hkv // hb),
        in_specs=in_specs,
        out_specs=out_specs,
        out_shape=out_shape,
        input_output_aliases=io_alias,
        compiler_params=_cparams(("arbitrary", "arbitrary")),
        name="sample_attn",
    )(*args)


def _rope_tables(pos):
    half = HEAD_DIM // 2
    inv_freq = ROPE_THETA ** (-jnp.arange(half, dtype=F32) / half)
    ang = pos.astype(F32)[:, None] * inv_freq[None, :]
    cos, sin = jnp.cos(ang), jnp.sin(ang)
    idx = jnp.arange(LANES) % half
    first = (jnp.arange(LANES) % HEAD_DIM) < half
    c = cos[:, idx]
    s = sin[:, idx]
    return c, jnp.where(first[None, :], -s, 0.0), jnp.where(first[None, :], 0.0, s)


def _tile_gain(g, heads):
    return jnp.tile(g.astype(F32), heads)


def kernel(x_prompt, x_sample, cache_a_kv_w128, cache_a_kv_w512, cache_a_kv_w2048, cache_b_kv,
           state_ffn_conv, attn_norm_g, ffn_norm_g, a_w_in, a_q_norm_g, a_k_norm_g, a_w_out,
           b_kv_norm_g, b_w_kv, b_k_norm_g, b_w_q, b_q_norm_g, b_sinks, b_w_out,
           ffn_w_up, ffn_conv_w, ffn_conv_b, ffn_w_down):
    batch, seq, _ = x_prompt.shape
    n_s = x_sample.shape[0]
    depth = attn_norm_g.shape[0]
    n_a = a_w_in.shape[0]
    scale = HEAD_DIM ** -0.5

    a_w_in_b = a_w_in.astype(BF16)
    a_w_out_b = a_w_out.astype(BF16)
    b_w_kv_b = b_w_kv.astype(BF16)
    b_w_q_b = b_w_q.astype(BF16)
    b_w_out_b = b_w_out.astype(BF16)
    w_up_b = ffn_w_up.astype(BF16)
    w_down_b = ffn_w_down.astype(BF16)

    tabs_p = _rope_tables(jnp.arange(seq))
    tabs_s = _rope_tables(jnp.full((n_s,), PAST_LEN))

    def a_gains(layer):
        gq = jnp.concatenate([_tile_gain(a_q_norm_g[layer, g], A_HEADS) for g in range(N_GROUPS)]) * scale
        gk = jnp.concatenate([_tile_gain(a_k_norm_g[layer, g], A_HEADS) for g in range(N_GROUPS)])
        return jnp.concatenate([gq, gk]).reshape(1, -1)

    kv_gain = _tile_gain(b_k_norm_g, B_KV_HEADS).reshape(1, -1)

    caches_t = [jnp.transpose(c, (0, 1, 2, 4, 5, 3))
                for c in (cache_a_kv_w128, cache_a_kv_w512, cache_a_kv_w2048)]
    cache_b_t = jnp.transpose(cache_b_kv, (0, 1, 3, 4, 2))[None]

    hp = x_prompt
    hs = x_sample.reshape(n_s, D_MODEL)
    kvt_p = None
    new_caches = [None] * N_GROUPS
    conv_p, conv_s = [], []
    kv_p = None
    new_b = None
    kv_st = None
    n_q = N_GROUPS * A_WIDTH

    for layer in range(depth):
        if layer < n_a:
            gains = a_gains(layer)
            qkv, kvt_p = _proj(hp.reshape(batch * seq, D_MODEL), attn_norm_g[layer], a_w_in_b[layer],
                               tabs_p, gains, 2 * n_q // LANES, 256, seq // 256, kvt_layer=layer,
                               kvt_layers=n_a, kvt_prev=kvt_p)
            o_p = _a_attn_prompt(qkv.reshape(batch, seq, 3 * n_q))
            w_out_l = a_w_out_b[layer]
            qkv_s = _proj(hs, attn_norm_g[layer], a_w_in_b[layer], tabs_s, gains, 2 * n_q // LANES,
                          n_s, 1)
            qkv_st = qkv_s.T
            ots, lts = [], []
            for g in range(N_GROUPS):
                qt = qkv_st[g * A_WIDTH:(g + 1) * A_WIDTH]
                kt = qkv_st[n_q + g * A_WIDTH:n_q + (g + 1) * A_WIDTH]
                vt = qkv_st[2 * n_q + g * A_WIDTH:2 * n_q + (g + 1) * A_WIDTH]
                ot, lt, new_caches[g] = _sample_attn(
                    qt, kt, vt, caches_t[g], layer, rep=1, dil=A_DILS[g], wmin=0, has_new=True,
                    shift=True, prev_out=new_caches[g])
                ots.append(ot)
                lts.append(lt)
            ot_s = jnp.stack(ots)
            lt_s = jnp.stack(lts)
        else:
            j = layer - n_a
            q_gain = (_tile_gain(b_q_norm_g[j], B_HEADS) * scale).reshape(1, -1)
            rep_b = B_HEADS // B_KV_HEADS
            q_s = _proj(hs, attn_norm_g[layer], b_w_q_b[j], tabs_s, q_gain, B_HEADS * HEAD_DIM // LANES,
                        n_s, 1)
            if layer == n_a:
                kv_p = _proj(hp.reshape(batch * seq, D_MODEL), b_kv_norm_g, b_w_kv_b, tabs_p, kv_gain,
                             1, 256, seq // 256).reshape(batch, seq, 2 * LANES)
                kv_st = _proj(hs, b_kv_norm_g, b_w_kv_b, tabs_s, kv_gain, 1, n_s, 1).T
                ot, lt, new_b = _sample_attn(q_s.T, kv_st[:LANES], kv_st[LANES:], cache_b_t, 0, rep=rep_b,
                                             dil=1, wmin=1, has_new=True, shift=True, sinks=b_sinks[j])
            else:
                ot, lt = _sample_attn(q_s.T, kv_st[:LANES], kv_st[LANES:], new_b, 0, rep=rep_b, dil=1,
                                      wmin=0, has_new=False, shift=False, sinks=b_sinks[j])
            q_p = _proj(hp.reshape(batch * seq, D_MODEL), attn_norm_g[layer], b_w_q_b[j], tabs_p, q_gain,
                        B_HEADS * HEAD_DIM // LANES, 256, seq // 256).reshape(batch, seq, -1)
            o_p = _b_attn_prompt(q_p, kv_p, b_sinks[j])
            w_out_l = b_w_out_b[j]
            ot_s = ot[None]
            lt_s = lt[None]

        hp, tail = _ffn_prompt(hp, o_p, w_out_l, ffn_norm_g[layer], w_up_b[layer], ffn_conv_w[layer],
                               ffn_conv_b[layer], w_down_b[layer])
        conv_p.append(tail[:, 6:8, :])
        st0 = state_ffn_conv[layer, :, 0, :]
        st1 = state_ffn_conv[layer, :, 1, :]
        hs, u_s = _ffn_sample(hs, ot_s, lt_s, w_out_l, ffn_norm_g[layer], w_up_b[layer],
                              ffn_conv_w[layer], ffn_conv_b[layer], w_down_b[layer], st0, st1)
        conv_s.append(jnp.stack([st1, u_s], axis=1))

    a_out_p = [jnp.transpose(t.reshape(n_a, batch, 2, A_HEADS, HEAD_DIM, t.shape[-1]), (0, 1, 2, 5, 3, 4))
               for t in kvt_p]
    a_out_s = [jnp.transpose(c, (0, 1, 2, 5, 3, 4)) for c in new_caches]
    b_p = jnp.stack([kv_p[:, seq - B_WINDOW:, :LANES], kv_p[:, seq - B_WINDOW:, LANES:]], axis=1)
    b_p = b_p.reshape(batch, 2, B_WINDOW, B_KV_HEADS, HEAD_DIM)
    b_s = jnp.transpose(new_b[0], (0, 1, 4, 2, 3))
    return (hp, hs.reshape(n_s, 1, D_MODEL), a_out_p[0], a_out_s[0], a_out_p[1], a_out_s[1],
            a_out_p[2], a_out_s[2], b_p, b_s, jnp.stack(conv_p, axis=0), jnp.stack(conv_s, axis=0))
```

```python
import functools

import jax
import jax.numpy as jnp
from jax import lax
from jax.experimental import pallas as pl
from jax.experimental.pallas import tpu as pltpu

D_MODEL = 1024
HEAD_DIM = 64
N_GROUPS = 3
A_DILS = (1, 4, 16)
A_WINDOWS = (128, 512, 2048)
A_HEADS = 8
A_WIDTH = A_HEADS * HEAD_DIM
B_HEADS = 16
B_KV_HEADS = 2
B_WINDOW = 128
D_FF = 2816
ROPE_THETA = 10000.0
EPS = 1e-6
PAST_LEN = 8192
LANES = 128
BLK = 128
AVG_WIDTH = 256
NEG = -1e30
VMEM_LIMIT = 56 * 1024 * 1024
SAMPLE_BLOCK_BYTES = 8 * 1024 * 1024

F32 = jnp.float32
BF16 = jnp.bfloat16
_NT = (((1,), (1,)), ((), ()))


def _cparams(sem):
    return pltpu.CompilerParams(dimension_semantics=sem, vmem_limit_bytes=VMEM_LIMIT)


def _proj_kernel(h_ref, g_ref, w_ref, c_ref, s1_ref, s2_ref, gain_ref, bd_ref, *rest,
                 n_rope, chunk, n_kvt, n_alias, seq_tiles):
    o_ref = rest[n_alias]
    kvt_refs = rest[n_alias + 1:]
    assert len(kvt_refs) == n_kvt
    x = h_ref[...]
    ms = jnp.mean(x * x, axis=-1, keepdims=True)
    hn = (x * lax.rsqrt(ms + EPS) * g_ref[...]).astype(BF16)
    cos = c_ref[...]
    sin_lo = s1_ref[...]
    sin_hi = s2_ref[...]
    bd = bd_ref[...]
    d_out = o_ref.shape[-1]
    per = chunk // LANES
    n_chunks = d_out // chunk

    def main(c0):
        return jnp.dot(hn, w_ref[:, c0 * chunk:(c0 + 1) * chunk], preferred_element_type=F32)

    acc_next = main(0)
    for c0 in range(n_chunks):
        acc = acc_next
        acc_next = main(c0 + 1) if c0 + 1 < n_chunks else None
        wide = bd.shape[0]
        for j0 in range(0, chunk, wide):
            blk0 = (c0 * chunk + j0) // LANES
            xw = acc[:, j0:j0 + wide]
            if blk0 < n_rope:
                msq = jnp.dot((xw * xw).astype(BF16), bd, preferred_element_type=F32)
                yw = xw * lax.rsqrt(msq + EPS) * gain_ref[:, blk0 * LANES:blk0 * LANES + wide]
            for j in range(wide // LANES):
                blk = blk0 + j
                if blk < n_rope:
                    y = yw[:, j * LANES:(j + 1) * LANES]
                    xc = (y * cos + pltpu.roll(y, 96, 1) * sin_lo + pltpu.roll(y, 32, 1) * sin_hi)
                else:
                    xc = xw[:, j * LANES:(j + 1) * LANES]
                o_ref[:, blk * LANES:(blk + 1) * LANES] = xc

    tm = h_ref.shape[0]
    n_q = d_out // 3
    it = pl.program_id(0) % seq_tiles
    for g, ref in enumerate(kvt_refs):
        bw = ref.shape[-1]
        first = seq_tiles - max(A_WINDOWS[g] // tm, 1)

        def emit(g=g, ref=ref, bw=bw):
            for kv in range(2):
                c0 = (1 + kv) * n_q + g * A_WIDTH
                ref[0, 0, kv] = o_ref[:, c0:c0 + A_WIDTH].T[:, tm - bw:]
        if first == 0:
            emit()
        else:
            pl.when(it >= first)(emit)


def _proj(h2d, norm_g, w, tabs, gains, n_rope, tm, seq_tiles, kvt_layer=None, kvt_layers=0,
          kvt_prev=None):
    n, _ = h2d.shape
    d_out = w.shape[1]
    chunk = 512 if d_out % 512 == 0 else 256
    cos, sin_lo, sin_hi = tabs
    gw = pl.cdiv(n_rope * LANES, AVG_WIDTH) * AVG_WIDTH
    gains = jnp.pad(gains, ((0, 0), (0, gw - gains.shape[1])), constant_values=1.0)
    bd = ((jnp.arange(AVG_WIDTH)[:, None] // HEAD_DIM == jnp.arange(AVG_WIDTH)[None, :] // HEAD_DIM)
          .astype(F32) / HEAD_DIM).astype(BF16)
    tab_spec = pl.BlockSpec((tm, LANES), lambda i: (i % seq_tiles, 0))
    in_specs = [
        pl.BlockSpec((tm, D_MODEL), lambda i: (i, 0)),
        pl.BlockSpec((1, D_MODEL), lambda i: (0, 0)),
        pl.BlockSpec((D_MODEL, d_out), lambda i: (0, 0)),
        tab_spec, tab_spec, tab_spec,
        pl.BlockSpec((1, gw), lambda i: (0, 0)),
        pl.BlockSpec((AVG_WIDTH, AVG_WIDTH), lambda i: (0, 0)),
    ]
    args = [h2d, norm_g.reshape(1, D_MODEL), w, cos, sin_lo, sin_hi, gains, bd]
    out_specs = [pl.BlockSpec((tm, d_out), lambda i: (i, 0))]
    out_shape = [jax.ShapeDtypeStruct((n, d_out), F32)]
    io_alias = {}
    n_kvt = 0
    if kvt_layer is not None:
        n_kvt = N_GROUPS
        batch = n // (tm * seq_tiles)
        for g, win in enumerate(A_WINDOWS):
            bw = min(tm, win)
            first = seq_tiles - max(win // tm, 1)
            out_specs.append(pl.BlockSpec(
                (1, 1, 2, A_WIDTH, bw),
                lambda i, first=first: (kvt_layer, i // seq_tiles, 0, 0,
                                        jnp.maximum(i % seq_tiles - first, 0))))
            out_shape.append(jax.ShapeDtypeStruct((kvt_layers, batch, 2, A_WIDTH, win), F32))
        if kvt_prev is not None:
            for g in range(N_GROUPS):
                in_specs.append(pl.BlockSpec(memory_space=pl.ANY))
                io_alias[len(args)] = 1 + g
                args.append(kvt_prev[g])
    outs = pl.pallas_call(
        functools.partial(_proj_kernel, n_rope=n_rope, chunk=chunk, n_kvt=n_kvt,
                          n_alias=len(io_alias), seq_tiles=seq_tiles),
        grid=(n // tm,),
        in_specs=in_specs,
        out_specs=out_specs,
        out_shape=out_shape,
        input_output_aliases=io_alias,
        compiler_params=_cparams(("arbitrary",)),
        name="proj",
    )(*args)
    return (outs[0], list(outs[1:])) if n_kvt else outs[0]


NB = 8


def _band_bias(off, nk, max_dist):
    qi = lax.broadcasted_iota(jnp.int32, (BLK, 1), 0)
    kj = lax.broadcasted_iota(jnp.int32, (1, nk), 1)
    dist = off + qi - kj
    return jnp.where((dist >= 0) & (dist <= max_dist), 0.0, NEG).astype(F32)


def _init_bias(bias_scr, max_dist):
    bias_scr[0] = _band_bias(0, 2 * BLK, max_dist)
    bias_scr[1] = _band_bias(BLK, 2 * BLK, max_dist)
    qi = lax.broadcasted_iota(jnp.int32, (BLK, 1), 0)
    kj = lax.broadcasted_iota(jnp.int32, (1, 2 * BLK), 1)
    bias_scr[2] = jnp.where((kj >= BLK) & (kj - BLK <= qi), 0.0, NEG).astype(F32)


def _attend(blocks):
    lane = lax.broadcasted_iota(jnp.int32, (1, LANES), 1)
    lo = lane < HEAD_DIM
    hi = jnp.logical_not(lo)
    scores = []
    for q, k, _, _ in blocks:
        for sel in (lo, hi):
            qa = jnp.where(sel, q, 0.0).astype(BF16)
            scores.append(lax.dot_general(qa, k, _NT, preferred_element_type=F32))
    parts = []
    for bi, (_, _, v, bias) in enumerate(blocks):
        for a in range(2):
            s = scores[2 * bi + a] + bias
            m = jnp.max(s, axis=1, keepdims=True)
            e = jnp.exp(s - m)
            den = jnp.sum(e, axis=1, keepdims=True)
            pv = jnp.dot(e.astype(BF16), v, preferred_element_type=F32)
            parts.append((pv, den, m))
    outs = []
    for bi in range(len(blocks)):
        (pv0, d0, m0), (pv1, d1, m1) = parts[2 * bi], parts[2 * bi + 1]
        den = jnp.where(lo, d0, d1)
        o = jnp.where(lo, pv0, pv1) / den
        lse = jnp.where(lo, m0, m1) + jnp.log(den)
        outs.append((o, lse))
    return outs


def _a_attn_kernel(q0, q1, q2, k0, k1, k2, v0, v1, v2, o_ref, o_scr, l_scr, bias_scr, kb_scr, vb_scr):
    seq = q0.shape[1]

    @pl.when((pl.program_id(0) == 0) & (pl.program_id(1) == 0))
    def _():
        _init_bias(bias_scr, BLK)

    kb_scr[...] = k0[0].astype(BF16)
    vb_scr[...] = v0[0].astype(BF16)

    def g0_body(it, carry):
        blocks, starts = [], []
        for j in range(NB):
            i = it * NB + j
            qs = pl.multiple_of(i * BLK, BLK)
            ks_ = pl.multiple_of(jnp.maximum(i - 1, 0) * BLK, BLK)
            if j == 0:
                bias = jnp.where(it == 0, bias_scr[0], bias_scr[1])
            else:
                bias = bias_scr[1]
            blocks.append((q0[0, pl.ds(qs, BLK), :], kb_scr[pl.ds(ks_, 2 * BLK), :],
                           vb_scr[pl.ds(ks_, 2 * BLK), :], bias))
            starts.append(qs)
        for qs, (o, l) in zip(starts, _attend(blocks)):
            o_scr[0, pl.ds(qs, BLK), :] = o
            l_scr[0, pl.ds(qs, BLK), :] = l
        return carry
    lax.fori_loop(0, seq // BLK // NB, g0_body, 0)

    d1 = A_DILS[1]
    nb1 = seq // d1 // BLK

    spi = NB // nb1

    def g1_body(it, carry):
        blocks, starts = [], []
        for j in range(spi):
            r = it * spi + j
            for i in range(nb1):
                kb = max(i - 1, 0)
                blocks.append((q1[0, pl.ds(i * BLK * d1 + r, BLK, stride=d1), :],
                               k1[0, pl.ds(kb * BLK * d1 + r, 2 * BLK, stride=d1), :].astype(BF16),
                               v1[0, pl.ds(kb * BLK * d1 + r, 2 * BLK, stride=d1), :].astype(BF16),
                               bias_scr[min(i, 1)]))
                starts.append(i * BLK * d1 + r)
        for qs, (o, l) in zip(starts, _attend(blocks)):
            o_scr[1, pl.ds(qs, BLK, stride=d1), :] = o
            l_scr[1, pl.ds(qs, BLK, stride=d1), :] = l
        return carry
    lax.fori_loop(0, d1 // spi, g1_body, 0)

    d2 = A_DILS[2]

    def g2_body(it, carry):
        blocks = []
        for j in range(0, NB, 2):
            ra = it * NB + j
            rb = ra + 1
            kk = jnp.concatenate([k2[0, pl.ds(ra, BLK, stride=d2), :],
                                  k2[0, pl.ds(rb, BLK, stride=d2), :]], axis=0).astype(BF16)
            vv = jnp.concatenate([v2[0, pl.ds(ra, BLK, stride=d2), :],
                                  v2[0, pl.ds(rb, BLK, stride=d2), :]], axis=0).astype(BF16)
            blocks.append((q2[0, pl.ds(ra, BLK, stride=d2), :], kk, vv, bias_scr[0]))
            blocks.append((q2[0, pl.ds(rb, BLK, stride=d2), :], kk, vv, bias_scr[2]))
        for j, (o, l) in enumerate(_attend(blocks)):
            r = it * NB + j
            o_scr[2, pl.ds(r, BLK, stride=d2), :] = o
            l_scr[2, pl.ds(r, BLK, stride=d2), :] = l
        return carry
    lax.fori_loop(0, d2 // NB, g2_body, 0)

    rows = 256

    def merge_body(i, carry):
        rs = pl.multiple_of(i * rows, rows)
        la = l_scr[0, pl.ds(rs, rows), :]
        lb = l_scr[1, pl.ds(rs, rows), :]
        lc = l_scr[2, pl.ds(rs, rows), :]
        m = jnp.maximum(jnp.maximum(la, lb), lc)
        ea = jnp.exp(la - m)
        eb = jnp.exp(lb - m)
        ec = jnp.exp(lc - m)
        o = ea * o_scr[0, pl.ds(rs, rows), :]
        o = o + eb * o_scr[1, pl.ds(rs, rows), :]
        o = o + ec * o_scr[2, pl.ds(rs, rows), :]
        o_ref[0, pl.ds(rs, rows), :] = (o / (ea + eb + ec)).astype(o_ref.dtype)
        return carry
    lax.fori_loop(0, seq // rows, merge_body, 0)


def _a_attn_prompt(qkv):
    b, seq, _ = qkv.shape
    pairs = A_WIDTH // LANES
    nq = N_GROUPS * pairs

    def spec(base, g):
        return pl.BlockSpec((1, seq, LANES), lambda bi, p, base=base, g=g: (bi, 0, base + g * pairs + p))
    in_specs = ([spec(0, g) for g in range(N_GROUPS)] + [spec(nq, g) for g in range(N_GROUPS)]
                + [spec(2 * nq, g) for g in range(N_GROUPS)])
    return pl.pallas_call(
        _a_attn_kernel,
        grid=(b, pairs),
        in_specs=in_specs,
        out_specs=pl.BlockSpec((1, seq, LANES), lambda bi, p: (bi, 0, p)),
        out_shape=jax.ShapeDtypeStruct((b, seq, A_WIDTH), BF16),
        scratch_shapes=[pltpu.VMEM((N_GROUPS, seq, LANES), F32),
                        pltpu.VMEM((N_GROUPS, seq, LANES), F32),
                        pltpu.VMEM((3, BLK, 2 * BLK), F32),
                        pltpu.VMEM((seq, LANES), BF16), pltpu.VMEM((seq, LANES), BF16)],
        compiler_params=_cparams(("arbitrary", "arbitrary")),
        name="a_attn_prompt",
    )(*([qkv] * 9))


def _b_attn_kernel(sink_ref, q_ref, k_ref, v_ref, o_ref, kdup_scr, vdup_scr, bias_scr):
    seq = q_ref.shape[1]
    p = pl.program_id(1)
    c = p // (B_HEADS // B_KV_HEADS // 2)
    lane = lax.broadcasted_iota(jnp.int32, (1, LANES), 1)
    lo = lane < HEAD_DIM

    @pl.when((pl.program_id(0) == 0) & (p == 0))
    def _():
        _init_bias(bias_scr, B_WINDOW - 1)

    keep = (lane >= HEAD_DIM).astype(jnp.int32) == c
    kdup_scr[...] = jnp.where(keep, k_ref[0], pltpu.roll(k_ref[0], HEAD_DIM, 1)).astype(BF16)
    vdup_scr[...] = jnp.where(keep, v_ref[0], pltpu.roll(v_ref[0], HEAD_DIM, 1)).astype(BF16)
    sink = jnp.where(lo, sink_ref[0, 2 * p], sink_ref[0, 2 * p + 1])

    def body(it, carry):
        blocks, starts = [], []
        for j in range(NB):
            i = it * NB + j
            qs = pl.multiple_of(i * BLK, BLK)
            ks_ = pl.multiple_of(jnp.maximum(i - 1, 0) * BLK, BLK)
            if j == 0:
                bias = jnp.where(it == 0, bias_scr[0], bias_scr[1])
            else:
                bias = bias_scr[1]
            blocks.append((q_ref[0, pl.ds(qs, BLK), :], kdup_scr[pl.ds(ks_, 2 * BLK), :],
                           vdup_scr[pl.ds(ks_, 2 * BLK), :], bias))
            starts.append(qs)
        for qs, (o, l) in zip(starts, _attend(blocks)):
            gate = 1.0 / (1.0 + jnp.exp(sink - l))
            o_ref[0, pl.ds(qs, BLK), :] = (o * gate).astype(o_ref.dtype)
        return carry
    lax.fori_loop(0, seq // BLK // NB, body, 0)


def _b_attn_prompt(q, kv, sinks):
    b, seq, _ = q.shape
    pairs = B_HEADS * HEAD_DIM // LANES
    return pl.pallas_call(
        _b_attn_kernel,
        grid=(b, pairs),
        in_specs=[
            pl.BlockSpec(memory_space=pltpu.SMEM),
            pl.BlockSpec((1, seq, LANES), lambda bi, p: (bi, 0, p)),
            pl.BlockSpec((1, seq, LANES), lambda bi, p: (bi, 0, 0)),
            pl.BlockSpec((1, seq, LANES), lambda bi, p: (bi, 0, 1)),
        ],
        out_specs=pl.BlockSpec((1, seq, LANES), lambda bi, p: (bi, 0, p)),
        out_shape=jax.ShapeDtypeStruct((b, seq, B_HEADS * HEAD_DIM), BF16),
        scratch_shapes=[pltpu.VMEM((seq, LANES), BF16), pltpu.VMEM((seq, LANES), BF16),
                        pltpu.VMEM((3, BLK, 2 * BLK), F32)],
        compiler_params=_cparams(("arbitrary", "arbitrary")),
        name="b_attn_prompt",
    )(sinks.reshape(1, B_HEADS), q, kv, kv)


FF_CHUNK = 256
DOWN_GROUP = 3


def _silu_gate(cg, cv):
    return (cg / (1.0 + jnp.exp(-cg))) * cv


def _ffn_prompt_kernel(h_ref, o_ref, wo_ref, g_ref, wup_ref, cw_ref, cb_ref, wdn_ref,
                       hout_ref, tail_ref):
    i = pl.program_id(1)
    tm = h_ref.shape[1]

    @pl.when(i == 0)
    def _():
        tail_ref[...] = jnp.zeros_like(tail_ref)

    h2 = h_ref[0] + jnp.dot(o_ref[0], wo_ref[...], preferred_element_type=F32)
    ms = jnp.mean(h2 * h2, axis=-1, keepdims=True)
    hn = (h2 * lax.rsqrt(ms + EPS) * g_ref[...]).astype(BF16)
    row = lax.broadcasted_iota(jnp.int32, (tm, 1), 0)
    n_chunks = D_FF // FF_CHUNK

    def up(j):
        return [jnp.dot(hn, wup_ref[:, base + j * FF_CHUNK:base + (j + 1) * FF_CHUNK],
                        preferred_element_type=F32) for base in (0, D_FF)]

    row8 = row[0:8]
    acc = jnp.zeros((tm, D_MODEL), F32)
    us = up(0)
    pending = []
    for j in range(n_chunks):
        us_next = up(j + 1) if j + 1 < n_chunks else None
        parts = []
        for u, base in zip(us, (0, D_FF)):
            sl = slice(base + j * FF_CHUNK, base + (j + 1) * FF_CHUNK)
            prev = tail_ref[0, :, sl]
            p1 = prev[7:8, :]
            p2 = prev[6:7, :]
            r1 = pltpu.roll(u, 1, 0)
            r2 = pltpu.roll(u, 2, 0)
            u1 = jnp.concatenate([jnp.where(row8 == 0, p1, r1[0:8]), r1[8:]], axis=0)
            u2 = jnp.concatenate(
                [jnp.where(row8 == 0, p2, jnp.where(row8 == 1, p1, r2[0:8])), r2[8:]], axis=0)
            cwj = cw_ref[:, sl]
            parts.append(cwj[0:1, :] * u2 + cwj[1:2, :] * u1 + cwj[2:3, :] * u + cb_ref[:, sl])
            tail_ref[0, :, sl] = u[tm - 8:tm, :]
        pending.append(_silu_gate(parts[0], parts[1]).astype(BF16))
        if len(pending) == DOWN_GROUP or j + 1 == n_chunks:
            first = j + 1 - len(pending)
            act = pending[0] if len(pending) == 1 else jnp.concatenate(pending, axis=1)
            acc = acc + jnp.dot(act, wdn_ref[first * FF_CHUNK:(j + 1) * FF_CHUNK, :],
                                preferred_element_type=F32)
            pending = []
        us = us_next
    hout_ref[0] = h2 + acc


def _ffn_prompt(h, o, w_out, g, w_up, conv_w, conv_b, w_down, tm=512):
    b, seq, _ = h.shape
    ko = o.shape[-1]
    const = lambda bi, i: (0, 0)
    once = pl.Buffered(1)
    return pl.pallas_call(
        _ffn_prompt_kernel,
        grid=(b, seq // tm),
        in_specs=[
            pl.BlockSpec((1, tm, D_MODEL), lambda bi, i: (bi, i, 0)),
            pl.BlockSpec((1, tm, ko), lambda bi, i: (bi, i, 0)),
            pl.BlockSpec((ko, D_MODEL), const, pipeline_mode=once),
            pl.BlockSpec((1, D_MODEL), const),
            pl.BlockSpec((D_MODEL, 2 * D_FF), const, pipeline_mode=once),
            pl.BlockSpec((3, 2 * D_FF), const),
            pl.BlockSpec((1, 2 * D_FF), const),
            pl.BlockSpec((D_FF, D_MODEL), const, pipeline_mode=once),
        ],
        out_specs=[
            pl.BlockSpec((1, tm, D_MODEL), lambda bi, i: (bi, i, 0)),
            pl.BlockSpec((1, 8, 2 * D_FF), lambda bi, i: (bi, 0, 0)),
        ],
        out_shape=[jax.ShapeDtypeStruct((b, seq, D_MODEL), F32),
                   jax.ShapeDtypeStruct((b, 8, 2 * D_FF), F32)],
        compiler_params=_cparams(("arbitrary", "arbitrary")),
        name="ffn_prompt",
    )(h, o, w_out, g.reshape(1, D_MODEL), w_up, conv_w, conv_b.reshape(1, 2 * D_FF), w_down)


def _ffn_sample_kernel(h_ref, ot_ref, lt_ref, wo_ref, g_ref, wup_ref, cw_ref, cb_ref, wdn_ref,
                       st0_ref, st1_ref, hout_ref, u_ref, *, n_groups):
    if n_groups == 1:
        ot = ot_ref[0]
    else:
        m = lt_ref[0]
        for gi in range(1, n_groups):
            m = jnp.maximum(m, lt_ref[gi])
        es = [jnp.exp(lt_ref[gi] - m) for gi in range(n_groups)]
        tot = es[0]
        for gi in range(1, n_groups):
            tot = tot + es[gi]
        ot = (es[0] / tot) * ot_ref[0]
        for gi in range(1, n_groups):
            ot = ot + (es[gi] / tot) * ot_ref[gi]
    o = ot.T.astype(BF16)
    h2 = h_ref[...] + jnp.dot(o, wo_ref[...], preferred_element_type=F32)
    ms = jnp.mean(h2 * h2, axis=-1, keepdims=True)
    hn = (h2 * lax.rsqrt(ms + EPS) * g_ref[...]).astype(BF16)
    acc = jnp.zeros(h2.shape, F32)
    for j in range(D_FF // FF_CHUNK):
        parts = []
        for base in (0, D_FF):
            c0 = base + j * FF_CHUNK
            sl = slice(c0, c0 + FF_CHUNK)
            u = jnp.dot(hn, wup_ref[:, sl], preferred_element_type=F32)
            u_ref[:, sl] = u
            cwj = cw_ref[:, sl]
            parts.append(cwj[0:1, :] * st0_ref[:, sl] + cwj[1:2, :] * st1_ref[:, sl]
                         + cwj[2:3, :] * u + cb_ref[:, sl])
        act = _silu_gate(parts[0], parts[1]).astype(BF16)
        acc = acc + jnp.dot(act, wdn_ref[j * FF_CHUNK:(j + 1) * FF_CHUNK, :],
                            preferred_element_type=F32)
    hout_ref[...] = h2 + acc


def _ffn_sample(h, ot, lt, w_out, g, w_up, conv_w, conv_b, w_down, st0, st1):
    n = h.shape[0]
    n_groups = ot.shape[0]
    full = lambda a: pl.BlockSpec(a.shape, lambda i, nd=a.ndim: (0,) * nd)
    args = (h, ot, lt, w_out, g.reshape(1, D_MODEL), w_up, conv_w, conv_b.reshape(1, 2 * D_FF),
            w_down, st0, st1)
    return pl.pallas_call(
        functools.partial(_ffn_sample_kernel, n_groups=n_groups),
        grid=(1,),
        in_specs=[full(a) for a in args],
        out_specs=[pl.BlockSpec((n, D_MODEL), lambda i: (0, 0)),
                   pl.BlockSpec((n, 2 * D_FF), lambda i: (0, 0))],
        out_shape=[jax.ShapeDtypeStruct((n, D_MODEL), F32),
                   jax.ShapeDtypeStruct((n, 2 * D_FF), F32)],
        compiler_params=_cparams(("arbitrary",)),
        name="ffn_sample",
    )(*args)


def _sample_attn_kernel(*refs, nb, hb, rep, dil, wmin, has_new, shift, use_sink, aliased):
    refs = list(refs)
    sink_ref = refs.pop(0) if use_sink else None
    qt_ref, kt_ref, vt_ref, cache_ref = refs[:4]
    refs = refs[4:]
    if aliased:
        refs = refs[1:]
    ot_ref, lt_ref = refs[:2]
    newc_ref = refs[2] if shift else None

    nblk = pl.program_id(0)
    hc = pl.program_id(1)
    w = cache_ref.shape[-1]
    lane = lax.broadcasted_iota(jnp.int32, (1, LANES), 1)
    lanew = lax.broadcasted_iota(jnp.int32, (1, w), 1)
    valid = ((lanew & (dil - 1)) == 0) & (lanew >= wmin)
    last = lanew == (w - 1)
    q_rows = hb * rep * HEAD_DIM
    kv_rows = hb * HEAD_DIM
    q_base = pl.multiple_of(hc * q_rows, q_rows)
    kv_base = pl.multiple_of(hc * kv_rows, kv_rows)

    @pl.when((nblk == 0) & (hc == 0))
    def _():
        ot_ref[...] = jnp.zeros_like(ot_ref)
        lt_ref[...] = jnp.zeros_like(lt_ref)

    def per_sample(ni, carry):
        sel = lane == nblk * nb + ni

        def columns(ref, base, rows):
            return jnp.sum(jnp.where(sel, ref[pl.ds(base, rows), :], 0.0), axis=1, keepdims=True)

        qcols = columns(qt_ref, q_base, q_rows)
        if has_new:
            kcols = columns(kt_ref, kv_base, kv_rows)
            vcols = columns(vt_ref, kv_base, kv_rows)
        def attend(kt3, vt3, q3, k3n, v3n, sink2):
            g = q3.shape[0]
            s = jnp.where(valid, jnp.sum(kt3 * q3, axis=1), NEG)
            m = jnp.max(s, axis=1, keepdims=True)
            if has_new:
                s_new = jnp.sum(q3 * k3n, axis=1)
                m = jnp.maximum(m, s_new)
            e = jnp.exp(s - m)
            den = jnp.sum(e, axis=1, keepdims=True)
            o3 = jnp.sum(vt3 * e[:, None, :], axis=2, keepdims=True)
            if has_new:
                e_new = jnp.exp(s_new - m)
                den = den + e_new
                o3 = o3 + e_new[:, :, None] * v3n
            o3 = o3 / den[:, :, None]
            lse = m + jnp.log(den)
            if use_sink:
                o3 = o3 * (1.0 / (1.0 + jnp.exp(sink2 - lse)))[:, :, None]
            lse3 = jnp.broadcast_to(lse[:, :, None], (g, HEAD_DIM, 1))
            return o3.reshape(g * HEAD_DIM, 1), lse3.reshape(g * HEAD_DIM, 1)

        def put(base, rows, ocol, lcol):
            sl = pl.ds(pl.multiple_of(base, HEAD_DIM), rows)
            ot_ref[sl, :] = jnp.where(sel, ocol, ot_ref[sl, :])
            lt_ref[sl, :] = jnp.where(sel, lcol, lt_ref[sl, :])

        if rep == 1:
            kt3 = cache_ref[0, ni, 0]
            vt3 = cache_ref[0, ni, 1]
            k3n = kcols.reshape(hb, HEAD_DIM, 1) if has_new else None
            v3n = vcols.reshape(hb, HEAD_DIM, 1) if has_new else None
            sink2 = sink_ref[pl.ds(q_base // HEAD_DIM, hb), :] if use_sink else None
            ocol, lcol = attend(kt3, vt3, qcols.reshape(hb, HEAD_DIM, 1), k3n, v3n, sink2)
            put(q_base, q_rows, ocol, lcol)
            if shift:
                newc_ref[0, ni, 0] = jnp.where(last, k3n, pltpu.roll(kt3, w - 1, 2))
                newc_ref[0, ni, 1] = jnp.where(last, v3n, pltpu.roll(vt3, w - 1, 2))
            return carry
        for j in range(hb):
            kt = cache_ref[0, ni, 0, j]
            vt = cache_ref[0, ni, 1, j]
            kcol = kcols[j * HEAD_DIM:(j + 1) * HEAD_DIM] if has_new else None
            vcol = vcols[j * HEAD_DIM:(j + 1) * HEAD_DIM] if has_new else None
            q3 = qcols[j * rep * HEAD_DIM:(j + 1) * rep * HEAD_DIM].reshape(rep, HEAD_DIM, 1)
            head0 = q_base // HEAD_DIM + j * rep
            sink2 = sink_ref[pl.ds(head0, rep), :] if use_sink else None
            ocol, lcol = attend(kt[None], vt[None], q3, kcol[None] if has_new else None,
                                vcol[None] if has_new else None, sink2)
            put(q_base + j * rep * HEAD_DIM, rep * HEAD_DIM, ocol, lcol)
            if shift:
                newc_ref[0, ni, 0, j] = jnp.where(last, kcol, pltpu.roll(kt, w - 1, 1))
                newc_ref[0, ni, 1, j] = jnp.where(last, vcol, pltpu.roll(vt, w - 1, 1))
        return carry
    lax.fori_loop(0, nb, per_sample, 0, unroll=2 if nb % 2 == 0 else 1)


def _sample_attn(qt, kt, vt, cache, layer, *, rep, dil, wmin, has_new, shift, sinks=None,
                 prev_out=None):
    nl, n, _, hkv, _, w = cache.shape
    hq = hkv * rep
    per_sample_bytes = 2 * hkv * HEAD_DIM * w * 4
    hb = hkv if per_sample_bytes <= SAMPLE_BLOCK_BYTES else hkv // 2
    per_sample_bytes = 2 * hb * HEAD_DIM * w * 4
    nb = max(1, min(8, SAMPLE_BLOCK_BYTES // 2 // per_sample_bytes))
    use_sink = sinks is not None
    aliased = prev_out is not None
    const2 = lambda ni, hc: (0, 0)
    cache_spec = pl.BlockSpec((1, nb, 2, hb, HEAD_DIM, w), lambda ni, hc: (layer, ni, 0, hc, 0, 0))
    in_specs, args = [], []
    if use_sink:
        in_specs.append(pl.BlockSpec((hq, 1), const2))
        args.append(sinks.reshape(hq, 1))
    in_specs += [pl.BlockSpec((hq * HEAD_DIM, n), const2), pl.BlockSpec((hkv * HEAD_DIM, n), const2),
                 pl.BlockSpec((hkv * HEAD_DIM, n), const2), cache_spec]
    args += [qt, kt, vt, cache]
    io_alias = {}
    if aliased:
        in_specs.append(pl.BlockSpec(memory_space=pl.ANY))
        io_alias = {len(args): 2}
        args.append(prev_out)
    out_specs = [pl.BlockSpec((hq * HEAD_DIM, n), const2), pl.BlockSpec((hq * HEAD_DIM, n), const2)]
    out_shape = [jax.ShapeDtypeStruct((hq * HEAD_DIM, n), F32)] * 2
    if shift:
        out_specs.append(cache_spec)
        out_shape.append(jax.ShapeDtypeStruct(cache.shape, F32))
    return pl.pallas_call(
        functools.partial(_sample_attn_kernel, nb=nb, hb=hb, rep=rep, dil=dil, wmin=wmin,
                          has_new=has_new, shift=shift, use_sink=use_sink, aliased=aliased),
        grid=(n // nb, hkv // hb),
        in_specs=in_specs,
        out_specs=out_specs,
        out_shape=out_shape,
        input_output_aliases=io_alias,
        compiler_params=_cparams(("arbitrary", "arbitrary")),
        name="sample_attn",
    )(*args)


def _rope_tables(pos):
    half = HEAD_DIM // 2
    inv_freq = ROPE_THETA ** (-jnp.arange(half, dtype=F32) / half)
    ang = pos.astype(F32)[:, None] * inv_freq[None, :]
    cos, sin = jnp.cos(ang), jnp.sin(ang)
    idx = jnp.arange(LANES) % half
    first = (jnp.arange(LANES) % HEAD_DIM) < half
    c = cos[:, idx]
    s = sin[:, idx]
    return c, jnp.where(first[None, :], -s, 0.0), jnp.where(first[None, :], 0.0, s)


def _tile_gain(g, heads):
    return jnp.tile(g.astype(F32), heads)


def kernel(x_prompt, x_sample, cache_a_kv_w128, cache_a_kv_w512, cache_a_kv_w2048, cache_b_kv,
           state_ffn_conv, attn_norm_g, ffn_norm_g, a_w_in, a_q_norm_g, a_k_norm_g, a_w_out,
           b_kv_norm_g, b_w_kv, b_k_norm_g, b_w_q, b_q_norm_g, b_sinks, b_w_out,
           ffn_w_up, ffn_conv_w, ffn_conv_b, ffn_w_down):
    batch, seq, _ = x_prompt.shape
    n_s = x_sample.shape[0]
    depth = attn_norm_g.shape[0]
    n_a = a_w_in.shape[0]
    scale = HEAD_DIM ** -0.5

    a_w_in_b = a_w_in.astype(BF16)
    a_w_out_b = a_w_out.astype(BF16)
    b_w_kv_b = b_w_kv.astype(BF16)
    b_w_q_b = b_w_q.astype(BF16)
    b_w_out_b = b_w_out.astype(BF16)
    w_up_b = ffn_w_up.astype(BF16)
    w_down_b = ffn_w_down.astype(BF16)

    tabs_p = _rope_tables(jnp.arange(seq))
    tabs_s = _rope_tables(jnp.full((n_s,), PAST_LEN))

    def a_gains(layer):
        gq = jnp.concatenate([_tile_gain(a_q_norm_g[layer, g], A_HEADS) for g in range(N_GROUPS)]) * scale
        gk = jnp.concatenate([_tile_gain(a_k_norm_g[layer, g], A_HEADS) for g in range(N_GROUPS)])
        return jnp.concatenate([gq, gk]).reshape(1, -1)

    kv_gain = _tile_gain(b_k_norm_g, B_KV_HEADS).reshape(1, -1)

    caches_t = [jnp.transpose(c, (0, 1, 2, 4, 5, 3))
                for c in (cache_a_kv_w128, cache_a_kv_w512, cache_a_kv_w2048)]
    cache_b_t = jnp.transpose(cache_b_kv, (0, 1, 3, 4, 2))[None]

    hp = x_prompt
    hs = x_sample.reshape(n_s, D_MODEL)
    kvt_p = None
    new_caches = [None] * N_GROUPS
    conv_p, conv_s = [], []
    kv_p = None
    new_b = None
    kv_st = None
    n_q = N_GROUPS * A_WIDTH

    for layer in range(depth):
        if layer < n_a:
            gains = a_gains(layer)
            qkv, kvt_p = _proj(hp.reshape(batch * seq, D_MODEL), attn_norm_g[layer], a_w_in_b[layer],
                               tabs_p, gains, 2 * n_q // LANES, 256, seq // 256, kvt_layer=layer,
                               kvt_layers=n_a, kvt_prev=kvt_p)
            o_p = _a_attn_prompt(qkv.reshape(batch, seq, 3 * n_q))
            w_out_l = a_w_out_b[layer]
            qkv_s = _proj(hs, attn_norm_g[layer], a_w_in_b[layer], tabs_s, gains, 2 * n_q // LANES,
                          n_s, 1)
            qkv_st = qkv_s.T
            ots, lts = [], []
            for g in range(N_GROUPS):
                qt = qkv_st[g * A_WIDTH:(g + 1) * A_WIDTH]
                kt = qkv_st[n_q + g * A_WIDTH:n_q + (g + 1) * A_WIDTH]
                vt = qkv_st[2 * n_q + g * A_WIDTH:2 * n_q + (g + 1) * A_WIDTH]
                ot, lt, new_caches[g] = _sample_attn(
                    qt, kt, vt, caches_t[g], layer, rep=1, dil=A_DILS[g], wmin=0, has_new=True,
                    shift=True, prev_out=new_caches[g])
                ots.append(ot)
                lts.append(lt)
            ot_s = jnp.stack(ots)
            lt_s = jnp.stack(lts)
        else:
            j = layer - n_a
            q_gain = (_tile_gain(b_q_norm_g[j], B_HEADS) * scale).reshape(1, -1)
            rep_b = B_HEADS // B_KV_HEADS
            q_s = _proj(hs, attn_norm_g[layer], b_w_q_b[j], tabs_s, q_gain, B_HEADS * HEAD_DIM // LANES,
                        n_s, 1)
            if layer == n_a:
                kv_p = _proj(hp.reshape(batch * seq, D_MODEL), b_kv_norm_g, b_w_kv_b, tabs_p, kv_gain,
                             1, 256, seq // 256).reshape(batch, seq, 2 * LANES)
                kv_st = _proj(hs, b_kv_norm_g, b_w_kv_b, tabs_s, kv_gain, 1, n_s, 1).T
                ot, lt, new_b = _sample_attn(q_s.T, kv_st[:LANES], kv_st[LANES:], cache_b_t, 0, rep=rep_b,
                                             dil=1, wmin=1, has_new=True, shift=True, sinks=b_sinks[j])
            else:
                ot, lt = _sample_attn(q_s.T, kv_st[:LANES], kv_st[LANES:], new_b, 0, rep=rep_b, dil=1,
                                      wmin=0, has_new=False, shift=False, sinks=b_sinks[j])
            q_p = _proj(hp.reshape(batch * seq, D_MODEL), attn_norm_g[layer], b_w_q_b[j], tabs_p, q_gain,
                        B_HEADS * HEAD_DIM // LANES, 256, seq // 256).reshape(batch, seq, -1)
            o_p = _b_attn_prompt(q_p, kv_p, b_sinks[j])
            w_out_l = b_w_out_b[j]
            ot_s = ot[None]
            lt_s = lt[None]

        hp, tail = _ffn_prompt(hp, o_p, w_out_l, ffn_norm_g[layer], w_up_b[layer], ffn_conv_w[layer],
                               ffn_conv_b[layer], w_down_b[layer])
        conv_p.append(tail[:, 6:8, :])
        st0 = state_ffn_conv[layer, :, 0, :]
        st1 = state_ffn_conv[layer, :, 1, :]
        hs, u_s = _ffn_sample(hs, ot_s, lt_s, w_out_l, ffn_norm_g[layer], w_up_b[layer],
                              ffn_conv_w[layer], ffn_conv_b[layer], w_down_b[layer], st0, st1)
        conv_s.append(jnp.stack([st1, u_s], axis=1))

    a_out_p = [jnp.transpose(t.reshape(n_a, batch, 2, A_HEADS, HEAD_DIM, t.shape[-1]), (0, 1, 2, 5, 3, 4))
               for t in kvt_p]
    a_out_s = [jnp.transpose(c, (0, 1, 2, 5, 3, 4)) for c in new_caches]
    b_p = jnp.stack([kv_p[:, seq - B_WINDOW:, :LANES], kv_p[:, seq - B_WINDOW:, LANES:]], axis=1)
    b_p = b_p.reshape(batch, 2, B_WINDOW, B_KV_HEADS, HEAD_DIM)
    b_s = jnp.transpose(new_b[0], (0, 1, 4, 2, 3))
    return (hp, hs.reshape(n_s, 1, D_MODEL), a_out_p[0], a_out_s[0], a_out_p[1], a_out_s[1],
            a_out_p[2], a_out_s[2], b_p, b_s, jnp.stack(conv_p, axis=0), jnp.stack(conv_s, axis=0))
```
